```python
import jax, jax.numpy as jnp
from jax import lax
import numpy as np

D_MODEL = 2048
BATCH = 2
SEQ = 16384
DEPTH = 2

PLE_DIM = 256
D_FF = 5632
CONV_WIDTH = 512
CONV_K = 3
ATTN_HEADS = 16
ATTN_KV_HEADS = 2
HEAD_DIM = 64
WINDOW = 128
ATTN_BLOCK = 128
HGRN_HEADS = 8
HGRN_DK = 128
HGRN_DV = 64
HGRN_CHUNK = 64
NORM_EPS = 1e-6
NEG_BIG = -1e30
LB_FLOOR = 1e-20

IN_SPLITS = (CONV_WIDTH, CONV_WIDTH, CONV_WIDTH,
             ATTN_HEADS * HEAD_DIM, ATTN_KV_HEADS * HEAD_DIM, ATTN_KV_HEADS * HEAD_DIM,
             HGRN_HEADS * HGRN_DK, HGRN_HEADS * HGRN_DK, HGRN_HEADS * HGRN_DV, HGRN_HEADS * HGRN_DV,
             D_MODEL, D_MODEL, D_MODEL)
D_IN = sum(IN_SPLITS)

kernel_name = "hybrid_conv_swa_hgrn2_macaron_block"


def rmsnorm(x, w):
    xf = x.astype(jnp.float32)
    y = xf * lax.rsqrt(jnp.mean(xf * xf, axis=-1, keepdims=True) + NORM_EPS) * w.astype(jnp.float32)
    return y.astype(x.dtype)


def swiglu_half_step(x, norm_pre, w_gu, w_down, norm_post):
    h = rmsnorm(x, norm_pre)
    g, u = jnp.split(h @ w_gu, 2, axis=-1)
    y = (jax.nn.silu(g) * u) @ w_down
    return x + 0.5 * rmsnorm(y, norm_post)


def short_conv(xin, b_gate, c_gate, conv_w):
    u = c_gate * xin
    s = u.shape[1]
    y = conv_w[CONV_K - 1] * u
    for j in range(1, CONV_K):
        y = y + conv_w[CONV_K - 1 - j] * jnp.pad(u, ((0, 0), (j, 0), (0, 0)))[:, :s]
    return b_gate * y


def alibi_slopes(n_heads):
    return jnp.exp2(-8.0 * jnp.arange(1, n_heads + 1, dtype=jnp.float32) / n_heads)


def swa_attention(q, k, v, sinks):
    dtype = q.dtype
    b, s, _ = q.shape
    nb = s // ATTN_BLOCK
    g = ATTN_HEADS // ATTN_KV_HEADS
    qb = q.astype(jnp.float32).reshape(b, nb, ATTN_BLOCK, ATTN_KV_HEADS, g, HEAD_DIM)
    kb = k.astype(jnp.float32).reshape(b, nb, ATTN_BLOCK, ATTN_KV_HEADS, HEAD_DIM)
    vb = v.astype(jnp.float32).reshape(b, nb, ATTN_BLOCK, ATTN_KV_HEADS, HEAD_DIM)

    def with_prev(t):
        prev = jnp.concatenate([jnp.zeros_like(t[:, :1]), t[:, :-1]], axis=1)
        return jnp.concatenate([prev, t], axis=2)

    kk, vv = with_prev(kb), with_prev(vb)
    scores = jnp.einsum('bnqhgd,bnkhd->bnhgqk', qb, kk) * (HEAD_DIM ** -0.5)
    qi = jnp.arange(ATTN_BLOCK)
    ki = jnp.arange(2 * ATTN_BLOCK)
    dist = qi[:, None] + ATTN_BLOCK - ki[None, :]
    band = (dist >= 0) & (dist < WINDOW)
    key_pos = jnp.arange(nb)[:, None] * ATTN_BLOCK - ATTN_BLOCK + ki[None, :]
    mask = band[None] & (key_pos >= 0)[:, None, :]
    slopes = alibi_slopes(ATTN_HEADS).reshape(ATTN_KV_HEADS, g)
    scores = scores - slopes[:, :, None, None] * dist.astype(jnp.float32)
    scores = jnp.where(mask[None, :, None, None], scores, NEG_BIG)
    sink = sinks.astype(jnp.float32).reshape(ATTN_KV_HEADS, g)[:, :, None]
    m = jnp.maximum(scores.max(axis=-1), sink)
    pr = jnp.exp(scores - m[..., None])
    denom = pr.sum(axis=-1) + jnp.exp(sink - m)
    o = jnp.einsum('bnhgqk,bnkhd->bnqhgd', pr / denom[..., None], vv)
    return o.reshape(b, s, ATTN_HEADS * HEAD_DIM).astype(dtype)


def hgrn2(q, f_logit, i_in, g_out, lb, norm_w):
    dtype = q.dtype
    b, s, _ = q.shape
    nc = s // HGRN_CHUNK

    def to_chunks(t, d):
        t = t.astype(jnp.float32).reshape(b, nc, HGRN_CHUNK, HGRN_HEADS, d)
        return jnp.transpose(t, (1, 0, 3, 2, 4))

    qc = to_chunks(jax.nn.silu(q), HGRN_DK)
    z = to_chunks(f_logit, HGRN_DK)
    vc = to_chunks(i_in, HGRN_DV)
    lbh = lb.astype(jnp.float32).reshape(HGRN_HEADS, 1, HGRN_DK)
    log_lb = jnp.log(jnp.maximum(lbh, LB_FLOOR))
    log_f = jnp.logaddexp(jax.nn.log_sigmoid(z), log_lb + jax.nn.log_sigmoid(-z))
    kc = (1.0 - lbh) * jax.nn.sigmoid(-z)
    causal = jnp.tril(jnp.ones((HGRN_CHUNK, HGRN_CHUNK), dtype=bool))

    def step(state, xs):
        q_t, lf_t, k_t, v_t = xs
        a = jnp.cumsum(lf_t, axis=-2)
        o_inter = jnp.einsum('bhtk,bhkv->bhtv', q_t * jnp.exp(a), state)
        dec = jnp.where(causal[:, :, None], a[..., :, None, :] - a[..., None, :, :], NEG_BIG)
        att = jnp.einsum('bhtk,bhtsk,bhsk->bhts', q_t, jnp.exp(dec), k_t)
        o = o_inter + jnp.einsum('bhts,bhsv->bhtv', att, v_t)
        a_last = a[..., -1:, :]
        new_state = jnp.exp(a_last[..., 0, :])[..., None] * state + jnp.einsum(
            'bhsk,bhsv->bhkv', k_t * jnp.exp(a_last - a), v_t)
        return new_state, o

    s0 = jnp.zeros((b, HGRN_HEADS, HGRN_DK, HGRN_DV), jnp.float32)
    _, o = lax.scan(step, s0, (qc, log_f, kc, vc))
    o = jnp.transpose(o, (1, 0, 3, 2, 4)).reshape(b, s, HGRN_HEADS, HGRN_DV).astype(dtype)
    o = rmsnorm(o, norm_w).reshape(b, s, HGRN_HEADS * HGRN_DV)
    return o * jax.nn.silu(g_out)


def setup_inputs(seed: int = 0) -> dict:
    key = jax.random.key(seed)
    ks = jax.random.split(key, 25)
    f32 = jnp.float32

    def nrm(k, shape, scale):
        return scale * jax.random.normal(k, shape, f32)

    def gain(k, n):
        return 1.0 + 0.05 * jax.random.normal(k, (DEPTH, n), f32)

    return {
        "x": nrm(ks[0], (BATCH, SEQ, D_MODEL), 1.0),
        "p": nrm(ks[1], (DEPTH, BATCH, SEQ, PLE_DIM), 1.0),
        "ffn1_norm_pre": gain(ks[2], D_MODEL),
        "ffn1_w_gu": nrm(ks[3], (DEPTH, D_MODEL, 2 * D_FF), D_MODEL ** -0.5),
        "ffn1_w_down": nrm(ks[4], (DEPTH, D_FF, D_MODEL), D_FF ** -0.5),
        "ffn1_norm_post": gain(ks[5], D_MODEL),
        "mix_norm_pre": gain(ks[6], D_MODEL),
        "w_in": nrm(ks[7], (DEPTH, D_MODEL, D_IN), D_MODEL ** -0.5),
        "conv_w": nrm(ks[8], (DEPTH, CONV_K, CONV_WIDTH), CONV_K ** -0.5),
        "attn_sinks": nrm(ks[9], (DEPTH, ATTN_HEADS), 0.5),
        "hgrn_lb_logits": nrm(ks[10], (DEPTH, HGRN_HEADS * HGRN_DK), 0.5),
        "hgrn_norm": gain(ks[11], HGRN_DV),
        "w_branch_conv": nrm(ks[12], (DEPTH, CONV_WIDTH, D_MODEL), CONV_WIDTH ** -0.5),
        "w_branch_attn": nrm(ks[13], (DEPTH, ATTN_HEADS * HEAD_DIM, D_MODEL), (ATTN_HEADS * HEAD_DIM) ** -0.5),
        "w_branch_hgrn": nrm(ks[14], (DEPTH, HGRN_HEADS * HGRN_DV, D_MODEL), (HGRN_HEADS * HGRN_DV) ** -0.5),
        "w_o": nrm(ks[15], (DEPTH, D_MODEL, D_MODEL), D_MODEL ** -0.5),
        "mix_norm_post": gain(ks[16], D_MODEL),
        "ffn2_norm_pre": gain(ks[17], D_MODEL),
        "ffn2_w_gu": nrm(ks[18], (DEPTH, D_MODEL, 2 * D_FF), D_MODEL ** -0.5),
        "ffn2_w_down": nrm(ks[19], (DEPTH, D_FF, D_MODEL), D_FF ** -0.5),
        "ffn2_norm_post": gain(ks[20], D_MODEL),
        "ple_norm_pre": gain(ks[21], D_MODEL),
        "w_ple_gate": nrm(ks[22], (DEPTH, D_MODEL, D_MODEL), D_MODEL ** -0.5),
        "w_ple_proj": nrm(ks[23], (DEPTH, PLE_DIM, D_MODEL), PLE_DIM ** -0.5),
        "ple_norm_post": gain(ks[24], D_MODEL),
    }


def reference(x, p, ffn1_norm_pre, ffn1_w_gu, ffn1_w_down, ffn1_norm_post,
              mix_norm_pre, w_in, conv_w, attn_sinks, hgrn_lb_logits, hgrn_norm,
              w_branch_conv, w_branch_attn, w_branch_hgrn, w_o, mix_norm_post,
              ffn2_norm_pre, ffn2_w_gu, ffn2_w_down, ffn2_norm_post,
              ple_norm_pre, w_ple_gate, w_ple_proj, ple_norm_post):
    lb_p = jax.nn.softmax(hgrn_lb_logits.astype(jnp.float32), axis=0)
    lb_all = jnp.cumsum(lb_p, axis=0) - lb_p
    offsets = [int(o) for o in np.cumsum(IN_SPLITS)[:-1]]
    for l in range(DEPTH):
        x = swiglu_half_step(x, ffn1_norm_pre[l], ffn1_w_gu[l], ffn1_w_down[l], ffn1_norm_post[l])
        h = rmsnorm(x, mix_norm_pre[l])
        (c_x, c_b, c_c, a_q, a_k, a_v, r_q, r_f, r_i, r_g,
         g_conv, g_attn, g_hgrn) = jnp.split(h @ w_in[l], offsets, axis=-1)
        y_conv = short_conv(c_x, c_b, c_c, conv_w[l]) @ w_branch_conv[l]
        y_attn = swa_attention(a_q, a_k, a_v, attn_sinks[l]) @ w_branch_attn[l]
        y_hgrn = hgrn2(r_q, r_f, r_i, r_g, lb_all[l], hgrn_norm[l]) @ w_branch_hgrn[l]
        merged = (jax.nn.sigmoid(g_conv) * y_conv + jax.nn.sigmoid(g_attn) * y_attn
                  + jax.nn.sigmoid(g_hgrn) * y_hgrn)
        x = x + rmsnorm(merged @ w_o[l], mix_norm_post[l])
        x = swiglu_half_step(x, ffn2_norm_pre[l], ffn2_w_gu[l], ffn2_w_down[l], ffn2_norm_post[l])
        hp = rmsnorm(x, ple_norm_pre[l])
        ple = jax.nn.sigmoid(hp @ w_ple_gate[l]) * (p[l] @ w_ple_proj[l])
        x = x + rmsnorm(ple, ple_norm_post[l])
    return x
```

```python
import functools

import jax
import jax.numpy as jnp
from jax import lax
from jax.experimental import pallas as pl
from jax.experimental.pallas import tpu as pltpu

_MXU_DTYPE = jnp.bfloat16
_F32 = jnp.float32

NORM_EPS = 1e-6
NEG_BIG = -1e30
LB_FLOOR = 1e-20

CONV_WIDTH = 512
CONV_K = 3
ATTN_HEADS = 16
ATTN_KV_HEADS = 2
HEAD_DIM = 64
ATTN_BLOCK = 128
HGRN_HEADS = 8
HGRN_DK = 128
HGRN_DV = 64

_LANES = 128
_HALF = 64

_HGRN_CHUNK = 32
_HGRN_SAFE_LOG_DECAY = 80.0

_FFN_TM = 512
_FFN_TF = 512
_MIX_TM = 512
_MIX_TN = 512
_PLE_TM = 512
_CONV_TM = 512
_ATTN_TM = 512
_HGRN_TM = 256

_VMEM_LIMIT = 56 * 1024 * 1024


def _pick(n, pref):
    t = min(n, pref)
    assert n % t == 0, (n, pref)
    return t


def _rmsnorm(xf, w):
    return xf * lax.rsqrt(jnp.mean(xf * xf, axis=-1, keepdims=True) + NORM_EPS) * w


def _dot(a, b):
    return jnp.dot(a, b, preferred_element_type=_F32)


def _dot_nt(a, b):
    return lax.dot_general(a, b, (((1,), (1,)), ((), ())), preferred_element_type=_F32)


def _dot_tn(a, b):
    return lax.dot_general(a, b, (((0,), (0,)), ((), ())), preferred_element_type=_F32)


def _params(*sem):
    return pltpu.CompilerParams(dimension_semantics=sem, vmem_limit_bytes=_VMEM_LIMIT)


def _const_spec(shape):
    nd = len(shape)
    return pl.BlockSpec(shape, lambda *_: (0,) * nd, pipeline_mode=pl.Buffered(1))


def _ffn_kernel(x_ref, npre_ref, wg_ref, wu_ref, wd_ref, npost_ref, o_ref, h_ref):
    j = pl.program_id(1)
    last = pl.num_programs(1) - 1

    @pl.when(j == 0)
    def _():
        h_ref[...] = _rmsnorm(x_ref[...], npre_ref[...]).astype(h_ref.dtype)

    h = h_ref[...]
    g = _dot(h, wg_ref[...])
    u = _dot(h, wu_ref[...])
    a = (g * jax.nn.sigmoid(g) * u).astype(_MXU_DTYPE)
    y = _dot(a, wd_ref[...])

    @pl.when(j == 0)
    def _():
        o_ref[...] = y

    @pl.when(j > 0)
    def _():
        o_ref[...] += y

    @pl.when(j == last)
    def _():
        o_ref[...] = x_ref[...] + 0.5 * _rmsnorm(o_ref[...], npost_ref[...])


def _ffn_half_step(x, norm_pre, w_gu, w_down, norm_post):
    m, d = x.shape
    f = w_down.shape[0]
    tm, tf = _pick(m, _FFN_TM), _pick(f, _FFN_TF)
    nf = f // tf
    return pl.pallas_call(
        _ffn_kernel,
        grid=(m // tm, nf),
        in_specs=[
            pl.BlockSpec((tm, d), lambda i, j: (i, 0)),
            pl.BlockSpec((1, d), lambda i, j: (0, 0)),
            pl.BlockSpec((d, tf), lambda i, j: (0, j)),
            pl.BlockSpec((d, tf), lambda i, j: (0, j + nf)),
            pl.BlockSpec((tf, d), lambda i, j: (j, 0)),
            pl.BlockSpec((1, d), lambda i, j: (0, 0)),
        ],
        out_specs=pl.BlockSpec((tm, d), lambda i, j: (i, 0)),
        out_shape=jax.ShapeDtypeStruct((m, d), _F32),
        scratch_shapes=[pltpu.VMEM((tm, d), _MXU_DTYPE)],
        compiler_params=_params("parallel", "arbitrary"),
        name="ffn_half_step",
    )(x, norm_pre.reshape(1, d), w_gu, w_gu, w_down, norm_post.reshape(1, d))


def _ple_kernel(x_ref, p_ref, npre_ref, wg_ref, wp_ref, npost_ref, o_ref):
    x = x_ref[...]
    hp = _rmsnorm(x, npre_ref[...]).astype(_MXU_DTYPE)
    gate = jax.nn.sigmoid(_dot(hp, wg_ref[...]))
    proj = _dot(p_ref[...].astype(_MXU_DTYPE), wp_ref[...])
    o_ref[...] = x + _rmsnorm(gate * proj, npost_ref[...])


def _ple_step(x, p, norm_pre, w_gate, w_proj, norm_post):
    m, d = x.shape
    dp = p.shape[1]
    tm = _pick(m, _PLE_TM)
    return pl.pallas_call(
        _ple_kernel,
        grid=(m // tm,),
        in_specs=[
            pl.BlockSpec((tm, d), lambda i: (i, 0)),
            pl.BlockSpec((tm, dp), lambda i: (i, 0)),
            _const_spec((1, d)),
            _const_spec((d, d)),
            _const_spec((dp, d)),
            _const_spec((1, d)),
        ],
        out_specs=pl.BlockSpec((tm, d), lambda i: (i, 0)),
        out_shape=jax.ShapeDtypeStruct((m, d), _F32),
        compiler_params=_params("parallel"),
        name="ple_step",
    )(x, p, norm_pre.reshape(1, d), w_gate, w_proj, norm_post.reshape(1, d))


def _conv_kernel(x_ref, npre_ref, w_ref, cw_ref, o_ref, carry_ref, *, seq_tiles):
    first = (pl.program_id(0) % seq_tiles) == 0
    tm = x_ref.shape[0]
    cwid = o_ref.shape[1]
    h = _rmsnorm(x_ref[...], npre_ref[...]).astype(_MXU_DTYPE)
    proj = _dot(h, w_ref[...])
    xin, b_gate, c_gate = proj[:, :cwid], proj[:, cwid:2 * cwid], proj[:, 2 * cwid:]
    u = c_gate * xin

    @pl.when(first)
    def _():
        carry_ref[...] = jnp.zeros_like(carry_ref)

    prev1 = carry_ref[7:8, :]
    prev2 = carry_ref[6:7, :]
    row = lax.broadcasted_iota(jnp.int32, u.shape, 0)
    u1 = jnp.where(row == 0, prev1, pltpu.roll(u, 1, 0))
    u2 = jnp.where(row == 0, prev2, jnp.where(row == 1, prev1, pltpu.roll(u, 2, 0)))
    cw = cw_ref[...]
    y = cw[2:3, :] * u + cw[1:2, :] * u1 + cw[0:1, :] * u2
    o_ref[...] = (b_gate * y).astype(o_ref.dtype)
    carry_ref[...] = u[tm - 8:, :]


def _conv_branch(x, norm_pre, w_in_conv, conv_w, seq):
    m, d = x.shape
    cwid = conv_w.shape[1]
    tm = _pick(seq, _CONV_TM)
    cw = jnp.zeros((8, cwid), _F32).at[:CONV_K].set(conv_w)
    return pl.pallas_call(
        functools.partial(_conv_kernel, seq_tiles=seq // tm),
        grid=(m // tm,),
        in_specs=[
            pl.BlockSpec((tm, d), lambda i: (i, 0)),
            _const_spec((1, d)),
            _const_spec((d, 3 * cwid)),
            _const_spec((8, cwid)),
        ],
        out_specs=pl.BlockSpec((tm, cwid), lambda i: (i, 0)),
        out_shape=jax.ShapeDtypeStruct((m, cwid), _MXU_DTYPE),
        scratch_shapes=[pltpu.VMEM((8, cwid), _F32)],
        compiler_params=_params("arbitrary"),
        name="conv_branch",
    )(x, norm_pre.reshape(1, d), w_in_conv, cw)


def _attn_kernel(sink_ref, x_ref, npre_ref, w_ref, o_ref, q_s, k_s, v_s, *, seq_tiles):
    blk = ATTN_BLOCK
    tm = x_ref.shape[0]
    nq = ATTN_HEADS * HEAD_DIM
    nkv = ATTN_KV_HEADS * HEAD_DIM
    first = (pl.program_id(0) % seq_tiles) == 0

    h = _rmsnorm(x_ref[...], npre_ref[...]).astype(_MXU_DTYPE)
    qkv = _dot(h, w_ref[...])
    q_s[...] = qkv[:, :nq]

    @pl.when(first)
    def _():
        k_s[0:blk, :] = jnp.zeros((blk, nkv), _F32)
        v_s[0:blk, :] = jnp.zeros((blk, nkv), _F32)

    k_s[blk:blk + tm, :] = qkv[:, nq:nq + nkv]
    v_s[blk:blk + tm, :] = qkv[:, nq + nkv:nq + 2 * nkv]

    lo = lax.broadcasted_iota(jnp.int32, (1, _LANES), 1) < _HALF
    qi = lax.broadcasted_iota(jnp.int32, (blk, 2 * blk), 0)
    ki = lax.broadcasted_iota(jnp.int32, (blk, 2 * blk), 1)
    dist = qi + blk - ki
    band = (dist >= 0) & (dist < blk)
    distf = dist.astype(_F32)
    scale = HEAD_DIM ** -0.5
    group = ATTN_HEADS // ATTN_KV_HEADS

    def block(b, carry):
        r0 = pl.multiple_of(b * blk, blk)
        kc = k_s[pl.ds(r0, 2 * blk), :]
        vc = v_s[pl.ds(r0, 2 * blk), :]
        kr = pltpu.roll(kc, _HALF, 1)
        vr = pltpu.roll(vc, _HALF, 1)
        k_dup = [jnp.where(lo, kc, kr).astype(_MXU_DTYPE), jnp.where(lo, kr, kc).astype(_MXU_DTYPE)]
        v_half = [
            [jnp.where(lo, vc, 0.0).astype(_MXU_DTYPE), jnp.where(lo, 0.0, vr).astype(_MXU_DTYPE)],
            [jnp.where(lo, vr, 0.0).astype(_MXU_DTYPE), jnp.where(lo, 0.0, vc).astype(_MXU_DTYPE)],
        ]
        has_prev = jnp.logical_not(jnp.logical_and(first, b == 0))
        valid = band & ((ki >= blk) | has_prev)
        for pair in range(ATTN_HEADS // 2):
            kv = (2 * pair) // group
            qp = q_s[pl.ds(r0, blk), pair * _LANES:(pair + 1) * _LANES]
            acc = None
            for half in range(2):
                head = 2 * pair + half
                slope = 2.0 ** (-8.0 * (head + 1) / ATTN_HEADS)
                qm = (jnp.where(lo, qp, 0.0) if half == 0 else jnp.where(lo, 0.0, qp)).astype(_MXU_DTYPE)
                s = _dot_nt(qm, k_dup[kv]) * scale - slope * distf
                s = jnp.where(valid, s, NEG_BIG)
                sink = sink_ref[head]
                mx = jnp.maximum(jnp.max(s, axis=-1, keepdims=True), sink)
                pr = jnp.exp(s - mx)
                denom = jnp.sum(pr, axis=-1, keepdims=True) + jnp.exp(sink - mx)
                pn = (pr / denom).astype(_MXU_DTYPE)
                o = _dot(pn, v_half[kv][half])
                acc = o if acc is None else acc + o
            o_ref[pl.ds(r0, blk), pair * _LANES:(pair + 1) * _LANES] = acc.astype(o_ref.dtype)
        return carry

    lax.fori_loop(0, tm // blk, block, 0)
    k_s[0:blk, :] = k_s[tm:tm + blk, :]
    v_s[0:blk, :] = v_s[tm:tm + blk, :]


def _attn_branch(x, norm_pre, w_in_attn, sinks, seq):
    m, d = x.shape
    nq = ATTN_HEADS * HEAD_DIM
    nkv = ATTN_KV_HEADS * HEAD_DIM
    tm = _pick(seq, _ATTN_TM)
    assert tm % ATTN_BLOCK == 0 and nkv == _LANES
    return pl.pallas_call(
        functools.partial(_attn_kernel, seq_tiles=seq // tm),
        grid=(m // tm,),
        in_specs=[
            pl.BlockSpec(memory_space=pltpu.SMEM),
            pl.BlockSpec((tm, d), lambda i: (i, 0)),
            _const_spec((1, d)),
            _const_spec((d, nq + 2 * nkv)),
        ],
        out_specs=pl.BlockSpec((tm, nq), lambda i: (i, 0)),
        out_shape=jax.ShapeDtypeStruct((m, nq), _MXU_DTYPE),
        scratch_shapes=[
            pltpu.VMEM((tm, nq), _F32),
            pltpu.VMEM((tm + ATTN_BLOCK, nkv), _F32),
            pltpu.VMEM((tm + ATTN_BLOCK, nkv), _F32),
        ],
        compiler_params=_params("arbitrary"),
        name="attn_branch",
    )(sinks, x, norm_pre.reshape(1, d), w_in_attn)


def _hgrn_kernel(x_ref, npre_ref, w_ref, lbl_ref, nw_ref, o_ref, pr_s, st_s, *, seq_tiles, layer):
    ch = _HGRN_CHUNK
    tm = x_ref.shape[0]
    nk = HGRN_HEADS * HGRN_DK
    nv = HGRN_HEADS * HGRN_DV
    first = (pl.program_id(0) % seq_tiles) == 0

    @pl.when(first)
    def _():
        st_s[...] = jnp.zeros_like(st_s)

    h = _rmsnorm(x_ref[...], npre_ref[...]).astype(_MXU_DTYPE)
    pr_s[...] = _dot(h, w_ref[...])

    lg = lbl_ref[...]
    e = jnp.exp(lg - jnp.max(lg, axis=0, keepdims=True))
    sm = e / jnp.sum(e, axis=0, keepdims=True)
    lb = jnp.zeros((1, nk), _F32)
    for prev in range(layer):
        lb = lb + sm[prev:prev + 1, :]
    lb_floor = jnp.maximum(lb, LB_FLOOR)
    one_minus_lb = 1.0 - lb

    lo = lax.broadcasted_iota(jnp.int32, (1, _LANES), 1) < _HALF
    rid = lax.broadcasted_iota(jnp.int32, (ch, nk), 0)
    tril = lax.broadcasted_iota(jnp.int32, (ch, ch), 0) >= lax.broadcasted_iota(jnp.int32, (ch, ch), 1)
    nw = nw_ref[...]

    def chunk(c, carry):
        rows = pl.ds(pl.multiple_of(c * ch, ch), ch)
        q = pr_s[rows, 0:nk]
        z = pr_s[rows, nk:2 * nk]
        qs = q * jax.nn.sigmoid(q)
        sig_neg = jax.nn.sigmoid(-z)
        log_f = jnp.log(jax.nn.sigmoid(z) + lb_floor * sig_neg)
        kk = one_minus_lb * sig_neg
        a = log_f
        step = 1
        while step < ch:
            a = a + jnp.where(rid >= step, pltpu.roll(a, step, 0), 0.0)
            step *= 2
        a_last = a[ch - 1:ch, :]
        q_dec = (qs * jnp.exp(a)).astype(_MXU_DTYPE)
        k_inv = (kk * jnp.exp(-a)).astype(_MXU_DTYPE)
        k_end = (kk * jnp.exp(a_last - a)).astype(_MXU_DTYPE)
        d_end = jnp.exp(a_last)
        for pair in range(HGRN_HEADS // 2):
            vp = pr_s[rows, 2 * nk + pair * _LANES:2 * nk + (pair + 1) * _LANES].astype(_MXU_DTYPE)
            gp = pr_s[rows, 2 * nk + nv + pair * _LANES:2 * nk + nv + (pair + 1) * _LANES]
            outs = []
            for half in range(2):
                head = 2 * pair + half
                sl = slice(head * HGRN_DK, (head + 1) * HGRN_DK)
                att = jnp.where(tril, _dot_nt(q_dec[:, sl], k_inv[:, sl]), 0.0).astype(_MXU_DTYPE)
                state = st_s[head]
                outs.append(_dot_nt(q_dec[:, sl], state.astype(_MXU_DTYPE)) + _dot(att, vp))
                st_s[head] = state * d_end[:, sl] + _dot_tn(vp, k_end[:, sl])
            o = jnp.where(lo, outs[0], outs[1])
            osq = o * o
            ms_lo = jnp.sum(jnp.where(lo, osq, 0.0), axis=-1, keepdims=True) * (1.0 / HGRN_DV)
            ms_hi = jnp.sum(jnp.where(lo, 0.0, osq), axis=-1, keepdims=True) * (1.0 / HGRN_DV)
            rinv = jnp.where(lo, lax.rsqrt(ms_lo + NORM_EPS), lax.rsqrt(ms_hi + NORM_EPS))
            on = o * rinv * nw[:, pair * _LANES:(pair + 1) * _LANES]
            o_ref[rows, pair * _LANES:(pair + 1) * _LANES] = (on * (gp * jax.nn.sigmoid(gp))).astype(o_ref.dtype)
        return carry

    lax.fori_loop(0, tm // ch, chunk, 0)


def _hgrn_branch(x, norm_pre, w_in_hgrn, lb_logits, norm_w, seq, layer):
    m, d = x.shape
    nk = HGRN_HEADS * HGRN_DK
    nv = HGRN_HEADS * HGRN_DV
    depth = lb_logits.shape[0]
    tm = _pick(seq, _HGRN_TM)
    assert HGRN_DK == _LANES and 2 * HGRN_DV == _LANES and tm % _HGRN_CHUNK == 0
    nw = jnp.tile(norm_w.reshape(1, HGRN_DV), (1, HGRN_HEADS))
    return pl.pallas_call(
        functools.partial(_hgrn_kernel, seq_tiles=seq // tm, layer=layer),
        grid=(m // tm,),
        in_specs=[
            pl.BlockSpec((tm, d), lambda i: (i, 0)),
            _const_spec((1, d)),
            _const_spec((d, 2 * nk + 2 * nv)),
            _const_spec((depth, nk)),
            _const_spec((1, nv)),
        ],
        out_specs=pl.BlockSpec((tm, nv), lambda i: (i, 0)),
        out_shape=jax.ShapeDtypeStruct((m, nv), _MXU_DTYPE),
        scratch_shapes=[
            pltpu.VMEM((tm, 2 * nk + 2 * nv), _F32),
            pltpu.VMEM((HGRN_HEADS, _LANES, HGRN_DK), _F32),
        ],
        compiler_params=_params("arbitrary"),
        name="hgrn_branch",
    )(x, norm_pre.reshape(1, d), w_in_hgrn, lb_logits, nw)


def _merge_kernel(x_ref, npre_ref, wgc_ref, wga_ref, wgh_ref, yc_ref, ya_ref, yh_ref,
                  wbc_ref, wba_ref, wbh_ref, wo_ref, npost_ref, o_ref, h_ref):
    j = pl.program_id(1)
    last = pl.num_programs(1) - 1

    @pl.when(j == 0)
    def _():
        h_ref[...] = _rmsnorm(x_ref[...], npre_ref[...]).astype(h_ref.dtype)

    h = h_ref[...]
    merged = (jax.nn.sigmoid(_dot(h, wgc_ref[...])) * _dot(yc_ref[...], wbc_ref[...])
              + jax.nn.sigmoid(_dot(h, wga_ref[...])) * _dot(ya_ref[...], wba_ref[...])
              + jax.nn.sigmoid(_dot(h, wgh_ref[...])) * _dot(yh_ref[...], wbh_ref[...]))
    y = _dot(merged.astype(_MXU_DTYPE), wo_ref[...])

    @pl.when(j == 0)
    def _():
        o_ref[...] = y

    @pl.when(j > 0)
    def _():
        o_ref[...] += y

    @pl.when(j == last)
    def _():
        o_ref[...] = x_ref[...] + _rmsnorm(o_ref[...], npost_ref[...])


def _merge_step(x, norm_pre, w_gates, y_conv, y_attn, y_hgrn, wb_conv, wb_attn, wb_hgrn, w_o, norm_post):
    m, d = x.shape
    tm, tn = _pick(m, _MIX_TM), _pick(d, _MIX_TN)
    nt = d // tn
    nc, na, nh = y_conv.shape[1], y_attn.shape[1], y_hgrn.shape[1]
    return pl.pallas_call(
        _merge_kernel,
        grid=(m // tm, nt),
        in_specs=[
            pl.BlockSpec((tm, d), lambda i, j: (i, 0)),
            pl.BlockSpec((1, d), lambda i, j: (0, 0)),
            pl.BlockSpec((d, tn), lambda i, j: (0, j)),
            pl.BlockSpec((d, tn), lambda i, j: (0, j + nt)),
            pl.BlockSpec((d, tn), lambda i, j: (0, j + 2 * nt)),
            pl.BlockSpec((tm, nc), lambda i, j: (i, 0)),
            pl.BlockSpec((tm, na), lambda i, j: (i, 0)),
            pl.BlockSpec((tm, nh), lambda i, j: (i, 0)),
            pl.BlockSpec((nc, tn), lambda i, j: (0, j)),
            pl.BlockSpec((na, tn), lambda i, j: (0, j)),
            pl.BlockSpec((nh, tn), lambda i, j: (0, j)),
            pl.BlockSpec((tn, d), lambda i, j: (j, 0)),
            pl.BlockSpec((1, d), lambda i, j: (0, 0)),
        ],
        out_specs=pl.BlockSpec((tm, d), lambda i, j: (i, 0)),
        out_shape=jax.ShapeDtypeStruct((m, d), _F32),
        scratch_shapes=[pltpu.VMEM((tm, d), _MXU_DTYPE)],
        compiler_params=_params("parallel", "arbitrary"),
        name="merge_step",
    )(x, norm_pre.reshape(1, d), w_gates, w_gates, w_gates, y_conv, y_attn, y_hgrn,
      wb_conv, wb_attn, wb_hgrn, w_o, norm_post.reshape(1, d))


def kernel(x, p, ffn1_norm_pre, ffn1_w_gu, ffn1_w_down, ffn1_norm_post, mix_norm_pre, w_in, conv_w, attn_sinks, hgrn_lb_logits, hgrn_norm, w_branch_conv, w_branch_attn, w_branch_hgrn, w_o, mix_norm_post, ffn2_norm_pre, ffn2_w_gu, ffn2_w_down, ffn2_norm_post, ple_norm_pre, w_ple_gate, w_ple_proj, ple_norm_post):
    batch, seq, d = x.shape
    depth = p.shape[0]
    m = batch * seq
    n_conv = 3 * CONV_WIDTH
    n_attn = (ATTN_HEADS + 2 * ATTN_KV_HEADS) * HEAD_DIM
    n_hgrn = 2 * HGRN_HEADS * (HGRN_DK + HGRN_DV)
    o_attn = n_conv
    o_hgrn = o_attn + n_attn
    o_gate = o_hgrn + n_hgrn

    def mx(w):
        return w.astype(_MXU_DTYPE)

    xf = x.reshape(m, d)
    for l in range(depth):
        xf = _ffn_half_step(xf, ffn1_norm_pre[l], mx(ffn1_w_gu[l]), mx(ffn1_w_down[l]), ffn1_norm_post[l])
        w = w_in[l]
        y_conv = _conv_branch(xf, mix_norm_pre[l], mx(w[:, :o_attn]), conv_w[l], seq)
        y_attn = _attn_branch(xf, mix_norm_pre[l], mx(w[:, o_attn:o_hgrn]), attn_sinks[l], seq)
        y_hgrn = _hgrn_branch(xf, mix_norm_pre[l], mx(w[:, o_hgrn:o_gate]), hgrn_lb_logits, hgrn_norm[l], seq, l)
        xf = _merge_step(xf, mix_norm_pre[l], mx(w[:, o_gate:]), y_conv, y_attn, y_hgrn,
                         mx(w_branch_conv[l]), mx(w_branch_attn[l]), mx(w_branch_hgrn[l]), mx(w_o[l]),
                         mix_norm_post[l])
        xf = _ffn_half_step(xf, ffn2_norm_pre[l], mx(ffn2_w_gu[l]), mx(ffn2_w_down[l]), ffn2_norm_post[l])
        xf = _ple_step(xf, p[l].reshape(m, -1), ple_norm_pre[l], mx(w_ple_gate[l]), mx(w_ple_proj[l]),
                       ple_norm_post[l])
    return xf.reshape(batch, seq, d)
```

```python
import functools

import jax
import jax.numpy as jnp
from jax import lax
from jax.experimental import pallas as pl
from jax.experimental.pallas import tpu as pltpu

_MXU_DTYPE = jnp.bfloat16
_F32 = jnp.float32

NORM_EPS = 1e-6
NEG_BIG = -1e30
LB_FLOOR = 1e-20

CONV_WIDTH = 512
CONV_K = 3
ATTN_HEADS = 16
ATTN_KV_HEADS = 2
HEAD_DIM = 64
ATTN_BLOCK = 128
HGRN_HEADS = 8
HGRN_DK = 128
HGRN_DV = 64

_LANES = 128
_HALF = 64

_HGRN_CHUNK = 32
_HGRN_SAFE_LOG_DECAY = 80.0

_FFN_TM = 1024
_FFN_RB = 512
_FFN_TF = 512
_MIX_TM = 512
_MIX_TN = 512
_PLE_TM = 512
_CONV_TM = 512
_ATTN_TM = 512
_HGRN_TM = 256
_HGRN_UNROLL = 2
_ATTN_SKEW = 6

_VMEM_LIMIT = 60 * 1024 * 1024


def _pick(n, pref):
    t = min(n, pref)
    assert n % t == 0, (n, pref)
    return t


def _rmsnorm(xf, w):
    return xf * lax.rsqrt(jnp.mean(xf * xf, axis=-1, keepdims=True) + NORM_EPS) * w


def _dot(a, b):
    return jnp.dot(a, b, preferred_element_type=_F32)


def _dot_nt(a, b):
    return lax.dot_general(a, b, (((1,), (1,)), ((), ())), preferred_element_type=_F32)


def _dot_tn(a, b):
    return lax.dot_general(a, b, (((0,), (0,)), ((), ())), preferred_element_type=_F32)


def _params(*sem):
    return pltpu.CompilerParams(dimension_semantics=sem, vmem_limit_bytes=_VMEM_LIMIT)


def _const_spec(shape):
    nd = len(shape)
    return pl.BlockSpec(shape, lambda *_: (0,) * nd, pipeline_mode=pl.Buffered(1))


def _ffn_kernel(x_ref, npre_ref, wg_ref, wu_ref, wd_ref, npost_ref, o_ref, h_ref, *, rb):
    j = pl.program_id(1)
    last = pl.num_programs(1) - 1
    tm = x_ref.shape[0]

    @pl.when(j == 0)
    def _():
        h_ref[...] = _rmsnorm(x_ref[...], npre_ref[...]).astype(h_ref.dtype)
        o_ref[...] = jnp.zeros_like(o_ref)

    for r in range(tm // rb):
        rows = slice(r * rb, (r + 1) * rb)
        h = h_ref[rows, :]
        g = _dot(h, wg_ref[...])
        u = _dot(h, wu_ref[...])
        a = (g * jax.nn.sigmoid(g) * u).astype(_MXU_DTYPE)
        o_ref[rows, :] += _dot(a, wd_ref[...])

    @pl.when(j == last)
    def _():
        o_ref[...] = x_ref[...] + 0.5 * _rmsnorm(o_ref[...], npost_ref[...])


def _ffn_half_step(x, norm_pre, w_gu, w_down, norm_post):
    m, d = x.shape
    f = w_down.shape[0]
    tm, tf = _pick(m, _FFN_TM), _pick(f, _FFN_TF)
    nf = f // tf
    return pl.pallas_call(
        functools.partial(_ffn_kernel, rb=min(tm, _FFN_RB)),
        grid=(m // tm, nf),
        in_specs=[
            pl.BlockSpec((tm, d), lambda i, j: (i, 0)),
            pl.BlockSpec((1, d), lambda i, j: (0, 0)),
            pl.BlockSpec((d, tf), lambda i, j: (0, j)),
            pl.BlockSpec((d, tf), lambda i, j: (0, j + nf)),
            pl.BlockSpec((tf, d), lambda i, j: (j, 0)),
            pl.BlockSpec((1, d), lambda i, j: (0, 0)),
        ],
        out_specs=pl.BlockSpec((tm, d), lambda i, j: (i, 0)),
        out_shape=jax.ShapeDtypeStruct((m, d), _F32),
        scratch_shapes=[pltpu.VMEM((tm, d), _MXU_DTYPE)],
        compiler_params=_params("parallel", "arbitrary"),
        name="ffn_half_step",
    )(x, norm_pre.reshape(1, d), w_gu, w_gu, w_down, norm_post.reshape(1, d))


def _ple_kernel(x_ref, p_ref, npre_ref, wg_ref, wp_ref, npost_ref, o_ref):
    x = x_ref[...]
    hp = _rmsnorm(x, npre_ref[...]).astype(_MXU_DTYPE)
    gate = jax.nn.sigmoid(_dot(hp, wg_ref[...]))
    proj = _dot(p_ref[...].astype(_MXU_DTYPE), wp_ref[...])
    o_ref[...] = x + _rmsnorm(gate * proj, npost_ref[...])


def _ple_step(x, p, norm_pre, w_gate, w_proj, norm_post):
    m, d = x.shape
    dp = p.shape[1]
    tm = _pick(m, _PLE_TM)
    return pl.pallas_call(
        _ple_kernel,
        grid=(m // tm,),
        in_specs=[
            pl.BlockSpec((tm, d), lambda i: (i, 0)),
            pl.BlockSpec((tm, dp), lambda i: (i, 0)),
            _const_spec((1, d)),
            _const_spec((d, d)),
            _const_spec((dp, d)),
            _const_spec((1, d)),
        ],
        out_specs=pl.BlockSpec((tm, d), lambda i: (i, 0)),
        out_shape=jax.ShapeDtypeStruct((m, d), _F32),
        compiler_params=_params("parallel"),
        name="ple_step",
    )(x, p, norm_pre.reshape(1, d), w_gate, w_proj, norm_post.reshape(1, d))


def _conv_kernel(x_ref, npre_ref, w_ref, cw_ref, o_ref, carry_ref, *, seq_tiles):
    first = (pl.program_id(0) % seq_tiles) == 0
    tm = x_ref.shape[0]
    cwid = o_ref.shape[1]
    h = _rmsnorm(x_ref[...], npre_ref[...]).astype(_MXU_DTYPE)
    proj = _dot(h, w_ref[...])
    xin, b_gate, c_gate = proj[:, :cwid], proj[:, cwid:2 * cwid], proj[:, 2 * cwid:]
    u = c_gate * xin

    @pl.when(first)
    def _():
        carry_ref[...] = jnp.zeros_like(carry_ref)

    prev1 = carry_ref[7:8, :]
    prev2 = carry_ref[6:7, :]
    row = lax.broadcasted_iota(jnp.int32, u.shape, 0)
    u1 = jnp.where(row == 0, prev1, pltpu.roll(u, 1, 0))
    u2 = jnp.where(row == 0, prev2, jnp.where(row == 1, prev1, pltpu.roll(u, 2, 0)))
    cw = cw_ref[...]
    y = cw[2:3, :] * u + cw[1:2, :] * u1 + cw[0:1, :] * u2
    o_ref[...] = (b_gate * y).astype(o_ref.dtype)
    carry_ref[...] = u[tm - 8:, :]


def _conv_branch(x, norm_pre, w_in_conv, conv_w, seq):
    m, d = x.shape
    cwid = conv_w.shape[1]
    tm = _pick(seq, _CONV_TM)
    cw = jnp.zeros((8, cwid), _F32).at[:CONV_K].set(conv_w)
    return pl.pallas_call(
        functools.partial(_conv_kernel, seq_tiles=seq // tm),
        grid=(m // tm,),
        in_specs=[
            pl.BlockSpec((tm, d), lambda i: (i, 0)),
            _const_spec((1, d)),
            _const_spec((d, 3 * cwid)),
            _const_spec((8, cwid)),
        ],
        out_specs=pl.BlockSpec((tm, cwid), lambda i: (i, 0)),
        out_shape=jax.ShapeDtypeStruct((m, cwid), _MXU_DTYPE),
        scratch_shapes=[pltpu.VMEM((8, cwid), _F32)],
        compiler_params=_params("arbitrary"),
        name="conv_branch",
    )(x, norm_pre.reshape(1, d), w_in_conv, cw)


def _attn_kernel(sink_ref, x_ref, npre_ref, w_ref, o_ref, q_s, k_s, v_s, bias_s, *, seq_tiles):
    blk = ATTN_BLOCK
    tm = x_ref.shape[0]
    nq = ATTN_HEADS * HEAD_DIM
    nkv = ATTN_KV_HEADS * HEAD_DIM
    first = (pl.program_id(0) % seq_tiles) == 0
    group = ATTN_HEADS // ATTN_KV_HEADS

    @pl.when(pl.program_id(0) == 0)
    def _():
        qi = lax.broadcasted_iota(jnp.int32, (blk, 2 * blk), 0)
        ki = lax.broadcasted_iota(jnp.int32, (blk, 2 * blk), 1)
        dist = qi + blk - ki
        band = (dist >= 0) & (dist < blk)
        distf = dist.astype(_F32)
        for head in range(ATTN_HEADS):
            slope = 2.0 ** (-8.0 * (head + 1) / ATTN_HEADS)
            bias_s[0, head] = jnp.where(band, -slope * distf, NEG_BIG)
            bias_s[1, head] = jnp.where(band & (ki >= blk), -slope * distf, NEG_BIG)

    h = _rmsnorm(x_ref[...], npre_ref[...]).astype(_MXU_DTYPE)
    qkv = _dot(h, w_ref[...])
    q_s[...] = (qkv[:, :nq] * (HEAD_DIM ** -0.5)).astype(q_s.dtype)

    @pl.when(first)
    def _():
        k_s[0:blk, :] = jnp.zeros((blk, nkv), _F32)
        v_s[0:blk, :] = jnp.zeros((blk, nkv), _F32)

    k_s[blk:blk + tm, :] = qkv[:, nq:nq + nkv]
    v_s[blk:blk + tm, :] = qkv[:, nq + nkv:nq + 2 * nkv]

    lo = lax.broadcasted_iota(jnp.int32, (1, _LANES), 1) < _HALF

    def block(b, carry):
        r0 = pl.multiple_of(b * blk, blk)
        kc = k_s[pl.ds(r0, 2 * blk), :]
        vc = v_s[pl.ds(r0, 2 * blk), :]
        kr = pltpu.roll(kc, _HALF, 1)
        vr = pltpu.roll(vc, _HALF, 1)
        k_dup = [jnp.where(lo, kc, kr).astype(_MXU_DTYPE), jnp.where(lo, kr, kc).astype(_MXU_DTYPE)]
        v_half = [
            [jnp.where(lo, vc, 0.0).astype(_MXU_DTYPE), jnp.where(lo, 0.0, vr).astype(_MXU_DTYPE)],
            [jnp.where(lo, vr, 0.0).astype(_MXU_DTYPE), jnp.where(lo, 0.0, vc).astype(_MXU_DTYPE)],
        ]
        no_prev = jnp.logical_and(first, b == 0).astype(jnp.int32)
        def scores(head):
            pair, half = divmod(head, 2)
            qp = q_s[pl.ds(r0, blk), pair * _LANES:(pair + 1) * _LANES]
            zero = jnp.zeros_like(qp)
            qm = jnp.where(lo, qp, zero) if half == 0 else jnp.where(lo, zero, qp)
            s = _dot_nt(qm, k_dup[head // group]) + bias_s[no_prev, head]
            sink = sink_ref[head]
            mx = jnp.maximum(jnp.max(s, axis=-1, keepdims=True), sink)
            pr = jnp.exp(s - mx)
            denom = jnp.sum(pr, axis=-1, keepdims=True) + jnp.exp(sink - mx)
            return pr.astype(_MXU_DTYPE), denom

        acc = {}

        def values(head, pr, denom):
            pair, half = divmod(head, 2)
            o = _dot(pr, v_half[head // group][half]) / denom
            if half == 0:
                acc[pair] = o
            else:
                o_ref[pl.ds(r0, blk), pair * _LANES:(pair + 1) * _LANES] = (acc.pop(pair) + o).astype(o_ref.dtype)

        pending = []
        for head in range(ATTN_HEADS):
            pending.append((head,) + scores(head))
            if len(pending) > _ATTN_SKEW:
                values(*pending.pop(0))
        while pending:
            values(*pending.pop(0))
        return carry

    lax.fori_loop(0, tm // blk, block, 0)
    k_s[0:blk, :] = k_s[tm:tm + blk, :]
    v_s[0:blk, :] = v_s[tm:tm + blk, :]


def _attn_branch(x, norm_pre, w_in_attn, sinks, seq):
    m, d = x.shape
    nq = ATTN_HEADS * HEAD_DIM
    nkv = ATTN_KV_HEADS * HEAD_DIM
    tm = _pick(seq, _ATTN_TM)
    assert tm % ATTN_BLOCK == 0 and nkv == _LANES
    return pl.pallas_call(
        functools.partial(_attn_kernel, seq_tiles=seq // tm),
        grid=(m // tm,),
        in_specs=[
            pl.BlockSpec(memory_space=pltpu.SMEM),
            pl.BlockSpec((tm, d), lambda i: (i, 0)),
            _const_spec((1, d)),
            _const_spec((d, nq + 2 * nkv)),
        ],
        out_specs=pl.BlockSpec((tm, nq), lambda i: (i, 0)),
        out_shape=jax.ShapeDtypeStruct((m, nq), _MXU_DTYPE),
        scratch_shapes=[
            pltpu.VMEM((tm, nq), _MXU_DTYPE),
            pltpu.VMEM((tm + ATTN_BLOCK, nkv), _F32),
            pltpu.VMEM((tm + ATTN_BLOCK, nkv), _F32),
            pltpu.VMEM((2, ATTN_HEADS, ATTN_BLOCK, 2 * ATTN_BLOCK), _F32),
        ],
        compiler_params=_params("arbitrary"),
        name="attn_branch",
    )(sinks, x, norm_pre.reshape(1, d), w_in_attn)


def _hgrn_kernel(x_ref, npre_ref, w_ref, lbl_ref, nw_ref, o_ref, pr_s, st_s, *, seq_tiles, layer):
    ch = _HGRN_CHUNK
    tm = x_ref.shape[0]
    nk = HGRN_HEADS * HGRN_DK
    nv = HGRN_HEADS * HGRN_DV
    first = (pl.program_id(0) % seq_tiles) == 0

    @pl.when(first)
    def _():
        st_s[...] = jnp.zeros_like(st_s)

    h = _rmsnorm(x_ref[...], npre_ref[...]).astype(_MXU_DTYPE)
    pr_s[...] = _dot(h, w_ref[...])

    lg = lbl_ref[...]
    e = jnp.exp(lg - jnp.max(lg, axis=0, keepdims=True))
    sm = e / jnp.sum(e, axis=0, keepdims=True)
    lb = jnp.zeros((1, nk), _F32)
    for prev in range(layer):
        lb = lb + sm[prev:prev + 1, :]
    lb_floor = jnp.maximum(lb, LB_FLOOR)
    one_minus_lb = 1.0 - lb

    lo = lax.broadcasted_iota(jnp.int32, (1, _LANES), 1) < _HALF
    rid = lax.broadcasted_iota(jnp.int32, (ch, nk), 0)
    tril = lax.broadcasted_iota(jnp.int32, (ch, ch), 0) >= lax.broadcasted_iota(jnp.int32, (ch, ch), 1)
    nw = nw_ref[...]

    def chunk(c, carry):
        rows = pl.ds(pl.multiple_of(c * ch, ch), ch)
        q = pr_s[rows, 0:nk]
        z = pr_s[rows, nk:2 * nk]
        qs = q * jax.nn.sigmoid(q)
        sig_neg = jax.nn.sigmoid(-z)
        log_f = jnp.log(jax.nn.sigmoid(z) + lb_floor * sig_neg)
        kk = one_minus_lb * sig_neg
        a = log_f
        step = 1
        while step < ch:
            a = a + jnp.where(rid >= step, pltpu.roll(a, step, 0), 0.0)
            step *= 2
        a_last = a[ch - 1:ch, :]
        q_dec = (qs * jnp.exp(a)).astype(_MXU_DTYPE)
        k_inv = (kk * jnp.exp(-a)).astype(_MXU_DTYPE)
        k_end = (kk * jnp.exp(a_last - a)).astype(_MXU_DTYPE)
        d_end = jnp.exp(a_last)
        heads = range(HGRN_HEADS)
        sls = [slice(hd * HGRN_DK, (hd + 1) * HGRN_DK) for hd in heads]
        vps = [pr_s[rows, 2 * nk + pr * _LANES:2 * nk + (pr + 1) * _LANES].astype(_MXU_DTYPE)
               for pr in range(HGRN_HEADS // 2)]
        att = [_dot_nt(q_dec[:, sls[hd]], k_inv[:, sls[hd]]) for hd in heads]
        upd = [_dot_tn(vps[hd // 2], k_end[:, sls[hd]]) for hd in heads]
        states = [st_s[hd] for hd in heads]
        inter = [_dot_nt(q_dec[:, sls[hd]], states[hd].astype(_MXU_DTYPE)) for hd in heads]
        intra = [_dot(jnp.where(tril, att[hd], 0.0).astype(_MXU_DTYPE), vps[hd // 2]) for hd in heads]
        for hd in heads:
            st_s[hd] = states[hd] * d_end[:, sls[hd]] + upd[hd]
        for pair in range(HGRN_HEADS // 2):
            gp = pr_s[rows, 2 * nk + nv + pair * _LANES:2 * nk + nv + (pair + 1) * _LANES]
            outs = [inter[2 * pair + half] + intra[2 * pair + half] for half in range(2)]
            o = jnp.where(lo, outs[0], outs[1])
            osq = o * o
            ms_lo = jnp.sum(jnp.where(lo, osq, 0.0), axis=-1, keepdims=True) * (1.0 / HGRN_DV)
            ms_hi = jnp.sum(jnp.where(lo, 0.0, osq), axis=-1, keepdims=True) * (1.0 / HGRN_DV)
            rinv = jnp.where(lo, lax.rsqrt(ms_lo + NORM_EPS), lax.rsqrt(ms_hi + NORM_EPS))
            on = o * rinv * nw[:, pair * _LANES:(pair + 1) * _LANES]
            o_ref[rows, pair * _LANES:(pair + 1) * _LANES] = (on * (gp * jax.nn.sigmoid(gp))).astype(o_ref.dtype)
        return carry

    lax.fori_loop(0, tm // ch, chunk, 0, unroll=_HGRN_UNROLL)


def _hgrn_branch(x, norm_pre, w_in_hgrn, lb_logits, norm_w, seq, layer):
    m, d = x.shape
    nk = HGRN_HEADS * HGRN_DK
    nv = HGRN_HEADS * HGRN_DV
    depth = lb_logits.shape[0]
    tm = _pick(seq, _HGRN_TM)
    assert HGRN_DK == _LANES and 2 * HGRN_DV == _LANES and tm % _HGRN_CHUNK == 0
    nw = jnp.tile(norm_w.reshape(1, HGRN_DV), (1, HGRN_HEADS))
    return pl.pallas_call(
        functools.partial(_hgrn_kernel, seq_tiles=seq // tm, layer=layer),
        grid=(m // tm,),
        in_specs=[
            pl.BlockSpec((tm, d), lambda i: (i, 0)),
            _const_spec((1, d)),
            _const_spec((d, 2 * nk + 2 * nv)),
            _const_spec((depth, nk)),
            _const_spec((1, nv)),
        ],
        out_specs=pl.BlockSpec((tm, nv), lambda i: (i, 0)),
        out_shape=jax.ShapeDtypeStruct((m, nv), _MXU_DTYPE),
        scratch_shapes=[
            pltpu.VMEM((tm, 2 * nk + 2 * nv), _F32),
            pltpu.VMEM((HGRN_HEADS, _LANES, HGRN_DK), _F32),
        ],
        compiler_params=_params("arbitrary"),
        name="hgrn_branch",
    )(x, norm_pre.reshape(1, d), w_in_hgrn, lb_logits, nw)


def _merge_kernel(x_ref, npre_ref, wgc_ref, wga_ref, wgh_ref, yc_ref, ya_ref, yh_ref,
                  wbc_ref, wba_ref, wbh_ref, wo_ref, npost_ref, o_ref, h_ref):
    j = pl.program_id(1)
    last = pl.num_programs(1) - 1

    @pl.when(j == 0)
    def _():
        h_ref[...] = _rmsnorm(x_ref[...], npre_ref[...]).astype(h_ref.dtype)
        o_ref[...] = jnp.zeros_like(o_ref)

    h = h_ref[...]
    merged = (jax.nn.sigmoid(_dot(h, wgc_ref[...])) * _dot(yc_ref[...], wbc_ref[...])
              + jax.nn.sigmoid(_dot(h, wga_ref[...])) * _dot(ya_ref[...], wba_ref[...])
              + jax.nn.sigmoid(_dot(h, wgh_ref[...])) * _dot(yh_ref[...], wbh_ref[...]))
    o_ref[...] += _dot(merged.astype(_MXU_DTYPE), wo_ref[...])

    @pl.when(j == last)
    def _():
        o_ref[...] = x_ref[...] + _rmsnorm(o_ref[...], npost_ref[...])


def _merge_step(x, norm_pre, w_gates, y_conv, y_attn, y_hgrn, wb_conv, wb_attn, wb_hgrn, w_o, norm_post):
    m, d = x.shape
    tm, tn = _pick(m, _MIX_TM), _pick(d, _MIX_TN)
    nt = d // tn
    nc, na, nh = y_conv.shape[1], y_attn.shape[1], y_hgrn.shape[1]
    return pl.pallas_call(
        _merge_kernel,
        grid=(m // tm, nt),
        in_specs=[
            pl.BlockSpec((tm, d), lambda i, j: (i, 0)),
            pl.BlockSpec((1, d), lambda i, j: (0, 0)),
            pl.BlockSpec((d, tn), lambda i, j: (0, j)),
            pl.BlockSpec((d, tn), lambda i, j: (0, j + nt)),
            pl.BlockSpec((d, tn), lambda i, j: (0, j + 2 * nt)),
            pl.BlockSpec((tm, nc), lambda i, j: (i, 0)),
            pl.BlockSpec((tm, na), lambda i, j: (i, 0)),
            pl.BlockSpec((tm, nh), lambda i, j: (i, 0)),
            pl.BlockSpec((nc, tn), lambda i, j: (0, j)),
            pl.BlockSpec((na, tn), lambda i, j: (0, j)),
            pl.BlockSpec((nh, tn), lambda i, j: (0, j)),
            pl.BlockSpec((tn, d), lambda i, j: (j, 0)),
            pl.BlockSpec((1, d), lambda i, j: (0, 0)),
        ],
        out_specs=pl.BlockSpec((tm, d), lambda i, j: (i, 0)),
        out_shape=jax.ShapeDtypeStruct((m, d), _F32),
        scratch_shapes=[pltpu.VMEM((tm, d), _MXU_DTYPE)],
        compiler_params=_params("parallel", "arbitrary"),
        name="merge_step",
    )(x, norm_pre.reshape(1, d), w_gates, w_gates, w_gates, y_conv, y_attn, y_hgrn,
      wb_conv, wb_attn, wb_hgrn, w_o, norm_post.reshape(1, d))


def kernel(x, p, ffn1_norm_pre, ffn1_w_gu, ffn1_w_down, ffn1_norm_post, mix_norm_pre, w_in, conv_w, attn_sinks, hgrn_lb_logits, hgrn_norm, w_branch_conv, w_branch_attn, w_branch_hgrn, w_o, mix_norm_post, ffn2_norm_pre, ffn2_w_gu, ffn2_w_down, ffn2_norm_post, ple_norm_pre, w_ple_gate, w_ple_proj, ple_norm_post):
    batch, seq, d = x.shape
    depth = p.shape[0]
    m = batch * seq
    n_conv = 3 * CONV_WIDTH
    n_attn = (ATTN_HEADS + 2 * ATTN_KV_HEADS) * HEAD_DIM
    n_hgrn = 2 * HGRN_HEADS * (HGRN_DK + HGRN_DV)
    o_attn = n_conv
    o_hgrn = o_attn + n_attn
    o_gate = o_hgrn + n_hgrn

    def mx(w):
        return w.astype(_MXU_DTYPE)

    xf = x.reshape(m, d)
    for l in range(depth):
        xf = _ffn_half_step(xf, ffn1_norm_pre[l], mx(ffn1_w_gu[l]), mx(ffn1_w_down[l]), ffn1_norm_post[l])
        w = w_in[l]
        y_conv = _conv_branch(xf, mix_norm_pre[l], mx(w[:, :o_attn]), conv_w[l], seq)
        y_attn = _attn_branch(xf, mix_norm_pre[l], mx(w[:, o_attn:o_hgrn]), attn_sinks[l], seq)
        y_hgrn = _hgrn_branch(xf, mix_norm_pre[l], mx(w[:, o_hgrn:o_gate]), hgrn_lb_logits, hgrn_norm[l], seq, l)
        xf = _merge_step(xf, mix_norm_pre[l], mx(w[:, o_gate:]), y_conv, y_attn, y_hgrn,
                         mx(w_branch_conv[l]), mx(w_branch_attn[l]), mx(w_branch_hgrn[l]), mx(w_o[l]),
                         mix_norm_post[l])
        xf = _ffn_half_step(xf, ffn2_norm_pre[l], mx(ffn2_w_gu[l]), mx(ffn2_w_down[l]), ffn2_norm_post[l])
        xf = _ple_step(xf, p[l].reshape(m, -1), ple_norm_pre[l], mx(w_ple_gate[l]), mx(w_ple_proj[l]),
                       ple_norm_post[l])
    return xf.reshape(batch, seq, d)
```

```python
import functools

import jax
import jax.numpy as jnp
from jax import lax
from jax.experimental import pallas as pl
from jax.experimental.pallas import tpu as pltpu

_MXU_DTYPE = jnp.bfloat16
_F32 = jnp.float32

NORM_EPS = 1e-6
NEG_BIG = -1e30
LB_FLOOR = 1e-20

CONV_WIDTH = 512
CONV_K = 3
ATTN_HEADS = 16
ATTN_KV_HEADS = 2
HEAD_DIM = 64
ATTN_BLOCK = 128
HGRN_HEADS = 8
HGRN_DK = 128
HGRN_DV = 64

_LANES = 128
_HALF = 64

_HGRN_CHUNK = 32
_HGRN_SAFE_LOG_DECAY = 80.0

_FFN_TM = 1024
_FFN_RB = 512
_NORM_ROWS = 128
_FFN_TF = 512
_MIX_TM = 512
_MIX_TN = 512
_PLE_TM = 1024
_CONV_TM = 512
_ROW_BLOCK = 256
_ATTN_TM = 512
_HGRN_TM = 256
_HGRN_UNROLL = 2
_ATTN_SKEW = 6

_VMEM_LIMIT = 60 * 1024 * 1024


def _pick(n, pref):
    t = min(n, pref)
    assert n % t == 0, (n, pref)
    return t


def _rmsnorm(xf, w):
    return xf * lax.rsqrt(jnp.mean(xf * xf, axis=-1, keepdims=True) + NORM_EPS) * w


def _rows_loop(n_rows, body):
    def step(r, carry):
        body(pl.ds(pl.multiple_of(r * _NORM_ROWS, _NORM_ROWS), _NORM_ROWS))
        return carry

    lax.fori_loop(0, n_rows // _NORM_ROWS, step, 0, unroll=2)


def _prenorm_rows(x_ref, w_ref, h_ref, o_ref):
    w = w_ref[...]

    def body(rows):
        h_ref[rows, :] = _rmsnorm(x_ref[rows, :], w).astype(h_ref.dtype)
        o_ref[rows, :] = jnp.zeros((_NORM_ROWS, o_ref.shape[1]), o_ref.dtype)

    _rows_loop(x_ref.shape[0], body)


def _residual_rows(x_ref, w_ref, o_ref, scale):
    w = w_ref[...] * scale

    def body(rows):
        o_ref[rows, :] = x_ref[rows, :] + _rmsnorm(o_ref[rows, :], w)

    _rows_loop(x_ref.shape[0], body)


def _dot(a, b):
    return jnp.dot(a, b, preferred_element_type=_F32)


def _dot_nt(a, b):
    return lax.dot_general(a, b, (((1,), (1,)), ((), ())), preferred_element_type=_F32)


def _dot_tn(a, b):
    return lax.dot_general(a, b, (((0,), (0,)), ((), ())), preferred_element_type=_F32)


def _params(*sem):
    return pltpu.CompilerParams(dimension_semantics=sem, vmem_limit_bytes=_VMEM_LIMIT)


def _const_spec(shape):
    nd = len(shape)
    return pl.BlockSpec(shape, lambda *_: (0,) * nd, pipeline_mode=pl.Buffered(1))


def _ffn_kernel(x_ref, npre_ref, wg_ref, wu_ref, wd_ref, npost_ref, o_ref, h_ref, *, rb):
    j = pl.program_id(1)
    last = pl.num_programs(1) - 1
    tm = x_ref.shape[0]

    @pl.when(j == 0)
    def _():
        _prenorm_rows(x_ref, npre_ref, h_ref, o_ref)

    for r in range(tm // rb):
        rows = slice(r * rb, (r + 1) * rb)
        h = h_ref[rows, :]
        g = _dot(h, wg_ref[...])
        u = _dot(h, wu_ref[...])
        a = (g * jax.nn.sigmoid(g) * u).astype(_MXU_DTYPE)
        o_ref[rows, :] += _dot(a, wd_ref[...])

    @pl.when(j == last)
    def _():
        _residual_rows(x_ref, npost_ref, o_ref, 0.5)


def _ffn_half_step(x, norm_pre, w_gu, w_down, norm_post):
    m, d = x.shape
    f = w_down.shape[0]
    tm, tf = _pick(m, _FFN_TM), _pick(f, _FFN_TF)
    nf = f // tf
    return pl.pallas_call(
        functools.partial(_ffn_kernel, rb=min(tm, _FFN_RB)),
        grid=(m // tm, nf),
        in_specs=[
            pl.BlockSpec((tm, d), lambda i, j: (i, 0)),
            pl.BlockSpec((1, d), lambda i, j: (0, 0)),
            pl.BlockSpec((d, tf), lambda i, j: (0, j)),
            pl.BlockSpec((d, tf), lambda i, j: (0, j + nf)),
            pl.BlockSpec((tf, d), lambda i, j: (j, 0)),
            pl.BlockSpec((1, d), lambda i, j: (0, 0)),
        ],
        out_specs=pl.BlockSpec((tm, d), lambda i, j: (i, 0)),
        out_shape=jax.ShapeDtypeStruct((m, d), _F32),
        scratch_shapes=[pltpu.VMEM((tm, d), _MXU_DTYPE)],
        compiler_params=_params("parallel", "arbitrary"),
        name="ffn_half_step",
    )(x, norm_pre.reshape(1, d), w_gu, w_gu, w_down, norm_post.reshape(1, d))


def _ple_kernel(x_ref, p_ref, npre_ref, wg_ref, wp_ref, npost_ref, o_ref, *, rb):
    for r in range(x_ref.shape[0] // rb):
        rows = slice(r * rb, (r + 1) * rb)
        x = x_ref[rows, :]
        hp = _rmsnorm(x, npre_ref[...]).astype(_MXU_DTYPE)
        gate = jax.nn.sigmoid(_dot(hp, wg_ref[...]))
        proj = _dot(p_ref[rows, :].astype(_MXU_DTYPE), wp_ref[...])
        o_ref[rows, :] = x + _rmsnorm(gate * proj, npost_ref[...])


def _ple_step(x, p, norm_pre, w_gate, w_proj, norm_post):
    m, d = x.shape
    dp = p.shape[1]
    tm = _pick(m, _PLE_TM)
    return pl.pallas_call(
        functools.partial(_ple_kernel, rb=min(tm, _ROW_BLOCK)),
        grid=(m // tm,),
        in_specs=[
            pl.BlockSpec((tm, d), lambda i: (i, 0)),
            pl.BlockSpec((tm, dp), lambda i: (i, 0)),
            _const_spec((1, d)),
            _const_spec((d, d)),
            _const_spec((dp, d)),
            _const_spec((1, d)),
        ],
        out_specs=pl.BlockSpec((tm, d), lambda i: (i, 0)),
        out_shape=jax.ShapeDtypeStruct((m, d), _F32),
        compiler_params=_params("parallel"),
        name="ple_step",
    )(x, p, norm_pre.reshape(1, d), w_gate, w_proj, norm_post.reshape(1, d))


def _conv_kernel(x_ref, npre_ref, w_ref, cw_ref, o_ref, carry_ref, *, seq_tiles, rb):
    first = (pl.program_id(0) % seq_tiles) == 0
    tm = x_ref.shape[0]
    cwid = o_ref.shape[1]

    @pl.when(first)
    def _():
        carry_ref[...] = jnp.zeros_like(carry_ref)

    cw = cw_ref[...]
    row = lax.broadcasted_iota(jnp.int32, (rb, cwid), 0)
    tail = carry_ref[...]
    for r in range(tm // rb):
        rows = slice(r * rb, (r + 1) * rb)
        h = _rmsnorm(x_ref[rows, :], npre_ref[...]).astype(_MXU_DTYPE)
        proj = _dot(h, w_ref[...])
        xin, b_gate, c_gate = proj[:, :cwid], proj[:, cwid:2 * cwid], proj[:, 2 * cwid:]
        u = c_gate * xin
        prev1 = tail[7:8, :]
        prev2 = tail[6:7, :]
        u1 = jnp.where(row == 0, prev1, pltpu.roll(u, 1, 0))
        u2 = jnp.where(row == 0, prev2, jnp.where(row == 1, prev1, pltpu.roll(u, 2, 0)))
        y = cw[2:3, :] * u + cw[1:2, :] * u1 + cw[0:1, :] * u2
        o_ref[rows, :] = (b_gate * y).astype(o_ref.dtype)
        tail = u[rb - 8:, :]
    carry_ref[...] = tail


def _conv_branch(x, norm_pre, w_in_conv, conv_w, seq):
    m, d = x.shape
    cwid = conv_w.shape[1]
    tm = _pick(seq, _CONV_TM)
    cw = jnp.zeros((8, cwid), _F32).at[:CONV_K].set(conv_w)
    return pl.pallas_call(
        functools.partial(_conv_kernel, seq_tiles=seq // tm, rb=min(tm, _ROW_BLOCK)),
        grid=(m // tm,),
        in_specs=[
            pl.BlockSpec((tm, d), lambda i: (i, 0)),
            _const_spec((1, d)),
            _const_spec((d, 3 * cwid)),
            _const_spec((8, cwid)),
        ],
        out_specs=pl.BlockSpec((tm, cwid), lambda i: (i, 0)),
        out_shape=jax.ShapeDtypeStruct((m, cwid), _MXU_DTYPE),
        scratch_shapes=[pltpu.VMEM((8, cwid), _F32)],
        compiler_params=_params("arbitrary"),
        name="conv_branch",
    )(x, norm_pre.reshape(1, d), w_in_conv, cw)


def _attn_kernel(sink_ref, x_ref, npre_ref, w_ref, o_ref, q_s, k_s, v_s, bias_s, *, seq_tiles):
    blk = ATTN_BLOCK
    tm = x_ref.shape[0]
    nq = ATTN_HEADS * HEAD_DIM
    nkv = ATTN_KV_HEADS * HEAD_DIM
    first = (pl.program_id(0) % seq_tiles) == 0
    group = ATTN_HEADS // ATTN_KV_HEADS

    @pl.when(pl.program_id(0) == 0)
    def _():
        qi = lax.broadcasted_iota(jnp.int32, (blk, 2 * blk), 0)
        ki = lax.broadcasted_iota(jnp.int32, (blk, 2 * blk), 1)
        dist = qi + blk - ki
        band = (dist >= 0) & (dist < blk)
        distf = dist.astype(_F32)
        for head in range(ATTN_HEADS):
            slope = 2.0 ** (-8.0 * (head + 1) / ATTN_HEADS)
            bias_s[0, head] = jnp.where(band, -slope * distf, NEG_BIG)
            bias_s[1, head] = jnp.where(band & (ki >= blk), -slope * distf, NEG_BIG)

    h = _rmsnorm(x_ref[...], npre_ref[...]).astype(_MXU_DTYPE)
    qkv = _dot(h, w_ref[...])
    q_s[...] = (qkv[:, :nq] * (HEAD_DIM ** -0.5)).astype(q_s.dtype)

    @pl.when(first)
    def _():
        k_s[0:blk, :] = jnp.zeros((blk, nkv), _F32)
        v_s[0:blk, :] = jnp.zeros((blk, nkv), _F32)

    k_s[blk:blk + tm, :] = qkv[:, nq:nq + nkv]
    v_s[blk:blk + tm, :] = qkv[:, nq + nkv:nq + 2 * nkv]

    lo = lax.broadcasted_iota(jnp.int32, (1, _LANES), 1) < _HALF

    def block(b, carry):
        r0 = pl.multiple_of(b * blk, blk)
        kc = k_s[pl.ds(r0, 2 * blk), :]
        vc = v_s[pl.ds(r0, 2 * blk), :]
        kr = pltpu.roll(kc, _HALF, 1)
        vr = pltpu.roll(vc, _HALF, 1)
        k_dup = [jnp.where(lo, kc, kr).astype(_MXU_DTYPE), jnp.where(lo, kr, kc).astype(_MXU_DTYPE)]
        v_half = [
            [jnp.where(lo, vc, 0.0).astype(_MXU_DTYPE), jnp.where(lo, 0.0, vr).astype(_MXU_DTYPE)],
            [jnp.where(lo, vr, 0.0).astype(_MXU_DTYPE), jnp.where(lo, 0.0, vc).astype(_MXU_DTYPE)],
        ]
        no_prev = jnp.logical_and(first, b == 0).astype(jnp.int32)
        def scores(head):
            pair, half = divmod(head, 2)
            qp = q_s[pl.ds(r0, blk), pair * _LANES:(pair + 1) * _LANES]
            zero = jnp.zeros_like(qp)
            qm = jnp.where(lo, qp, zero) if half == 0 else jnp.where(lo, zero, qp)
            s = _dot_nt(qm, k_dup[head // group]) + bias_s[no_prev, head]
            sink = sink_ref[head]
            mx = jnp.maximum(jnp.max(s, axis=-1, keepdims=True), sink)
            pr = jnp.exp(s - mx)
            denom = jnp.sum(pr, axis=-1, keepdims=True) + jnp.exp(sink - mx)
            return pr.astype(_MXU_DTYPE), denom

        acc = {}

        def values(head, pr, denom):
            pair, half = divmod(head, 2)
            o = _dot(pr, v_half[head // group][half]) / denom
            if half == 0:
                acc[pair] = o
            else:
                o_ref[pl.ds(r0, blk), pair * _LANES:(pair + 1) * _LANES] = (acc.pop(pair) + o).astype(o_ref.dtype)

        pending = []
        for head in range(ATTN_HEADS):
            pending.append((head,) + scores(head))
            if len(pending) > _ATTN_SKEW:
                values(*pending.pop(0))
        while pending:
            values(*pending.pop(0))
        return carry

    lax.fori_loop(0, tm // blk, block, 0)
    k_s[0:blk, :] = k_s[tm:tm + blk, :]
    v_s[0:blk, :] = v_s[tm:tm + blk, :]


def _attn_branch(x, norm_pre, w_in_attn, sinks, seq):
    m, d = x.shape
    nq = ATTN_HEADS * HEAD_DIM
    nkv = ATTN_KV_HEADS * HEAD_DIM
    tm = _pick(seq, _ATTN_TM)
    assert tm % ATTN_BLOCK == 0 and nkv == _LANES
    return pl.pallas_call(
        functools.partial(_attn_kernel, seq_tiles=seq // tm),
        grid=(m // tm,),
        in_specs=[
            pl.BlockSpec(memory_space=pltpu.SMEM),
            pl.BlockSpec((tm, d), lambda i: (i, 0)),
            _const_spec((1, d)),
            _const_spec((d, nq + 2 * nkv)),
        ],
        out_specs=pl.BlockSpec((tm, nq), lambda i: (i, 0)),
        out_shape=jax.ShapeDtypeStruct((m, nq), _MXU_DTYPE),
        scratch_shapes=[
            pltpu.VMEM((tm, nq), _MXU_DTYPE),
            pltpu.VMEM((tm + ATTN_BLOCK, nkv), _F32),
            pltpu.VMEM((tm + ATTN_BLOCK, nkv), _F32),
            pltpu.VMEM((2, ATTN_HEADS, ATTN_BLOCK, 2 * ATTN_BLOCK), _F32),
        ],
        compiler_params=_params("arbitrary"),
        name="attn_branch",
    )(sinks, x, norm_pre.reshape(1, d), w_in_attn)


def _hgrn_kernel(x_ref, npre_ref, w_ref, lbl_ref, nw_ref, o_ref, pr_s, st_s, bak_s, *, seq_tiles, layer):
    ch = _HGRN_CHUNK
    tm = x_ref.shape[0]
    nk = HGRN_HEADS * HGRN_DK
    nv = HGRN_HEADS * HGRN_DV
    first = (pl.program_id(0) % seq_tiles) == 0

    @pl.when(first)
    def _():
        st_s[...] = jnp.zeros_like(st_s)

    h = _rmsnorm(x_ref[...], npre_ref[...]).astype(_MXU_DTYPE)
    pr_s[...] = _dot(h, w_ref[...])

    lg = lbl_ref[...]
    e = jnp.exp(lg - jnp.max(lg, axis=0, keepdims=True))
    sm = e / jnp.sum(e, axis=0, keepdims=True)
    lb = jnp.zeros((1, nk), _F32)
    for prev in range(layer):
        lb = lb + sm[prev:prev + 1, :]
    lb_floor = jnp.maximum(lb, LB_FLOOR)
    one_minus_lb = 1.0 - lb

    lo = lax.broadcasted_iota(jnp.int32, (1, _LANES), 1) < _HALF
    rid = lax.broadcasted_iota(jnp.int32, (ch, nk), 0)
    tril = lax.broadcasted_iota(jnp.int32, (ch, ch), 0) >= lax.broadcasted_iota(jnp.int32, (ch, ch), 1)
    nw = nw_ref[...]

    def chunk(c, a_min, *, exact):
        rows = pl.ds(pl.multiple_of(c * ch, ch), ch)
        q = pr_s[rows, 0:nk]
        z = pr_s[rows, nk:2 * nk]
        qs = q * jax.nn.sigmoid(q)
        sig_pos = jax.nn.sigmoid(z)
        sig_neg = 1.0 - sig_pos
        log_f = jnp.log(sig_pos + lb_floor * sig_neg)
        kk = one_minus_lb * sig_neg
        a = log_f
        step = 1
        while step < ch:
            a = a + jnp.where(rid >= step, pltpu.roll(a, step, 0), 0.0)
            step *= 2
        a_last = a[ch - 1:ch, :]
        q_dec = (qs * jnp.exp(a)).astype(_MXU_DTYPE)
        k_end = (kk * jnp.exp(a_last - a)).astype(_MXU_DTYPE)
        d_end = jnp.exp(a_last)
        heads = range(HGRN_HEADS)
        sls = [slice(hd * HGRN_DK, (hd + 1) * HGRN_DK) for hd in heads]
        vps = [pr_s[rows, 2 * nk + pr * _LANES:2 * nk + (pr + 1) * _LANES] for pr in range(HGRN_HEADS // 2)]
        vpm = [v.astype(_MXU_DTYPE) for v in vps]
        if exact:
            sid = lax.broadcasted_iota(jnp.int32, (ch, nk), 0)
            sid_v = lax.broadcasted_iota(jnp.int32, (ch, _LANES), 0)

            def query_row(t, intra):
                a_t = jnp.sum(jnp.where(sid == t, a, 0.0), axis=0, keepdims=True)
                q_t = jnp.sum(jnp.where(sid == t, qs, 0.0), axis=0, keepdims=True)
                wts = q_t * jnp.exp(jnp.where(sid <= t, a_t - a, NEG_BIG)) * kk
                out = []
                for hd in heads:
                    col = jnp.sum(wts[:, sls[hd]], axis=-1, keepdims=True)
                    o_t = jnp.sum(col * vps[hd // 2], axis=0, keepdims=True)
                    out.append(jnp.where(sid_v == t, o_t, intra[hd]))
                return tuple(out)

            intra = lax.fori_loop(0, ch, query_row, tuple(jnp.zeros((ch, _LANES), _F32) for _ in heads))
        else:
            k_inv = (kk * jnp.exp(-a)).astype(_MXU_DTYPE)
            att = [_dot_nt(q_dec[:, sls[hd]], k_inv[:, sls[hd]]) for hd in heads]
        upd = [_dot_tn(vpm[hd // 2], k_end[:, sls[hd]]) for hd in heads]
        states = [st_s[hd] for hd in heads]
        inter = [_dot_nt(q_dec[:, sls[hd]], states[hd].astype(_MXU_DTYPE)) for hd in heads]
        if not exact:
            intra = [_dot(jnp.where(tril, att[hd], 0.0).astype(_MXU_DTYPE), vpm[hd // 2]) for hd in heads]
        for hd in heads:
            st_s[hd] = states[hd] * d_end[:, sls[hd]] + upd[hd]
        for pair in range(HGRN_HEADS // 2):
            gp = pr_s[rows, 2 * nk + nv + pair * _LANES:2 * nk + nv + (pair + 1) * _LANES]
            outs = [inter[2 * pair + half] + intra[2 * pair + half] for half in range(2)]
            o = jnp.where(lo, outs[0], outs[1])
            osq = o * o
            ms_lo = jnp.sum(jnp.where(lo, osq, 0.0), axis=-1, keepdims=True) * (1.0 / HGRN_DV)
            ms_hi = jnp.sum(jnp.where(lo, 0.0, osq), axis=-1, keepdims=True) * (1.0 / HGRN_DV)
            rinv = jnp.where(lo, lax.rsqrt(ms_lo + NORM_EPS), lax.rsqrt(ms_hi + NORM_EPS))
            on = o * rinv * nw[:, pair * _LANES:(pair + 1) * _LANES]
            o_ref[rows, pair * _LANES:(pair + 1) * _LANES] = (on * (gp * jax.nn.sigmoid(gp))).astype(o_ref.dtype)
        return jnp.minimum(a_min, a_last)

    bak_s[...] = st_s[...]
    a_min = lax.fori_loop(0, tm // ch, functools.partial(chunk, exact=False), jnp.zeros((1, nk), _F32),
                          unroll=_HGRN_UNROLL)

    @pl.when(jnp.min(a_min) < -_HGRN_SAFE_LOG_DECAY)
    def _():
        st_s[...] = bak_s[...]
        lax.fori_loop(0, tm // ch, functools.partial(chunk, exact=True), jnp.zeros((1, nk), _F32))


def _hgrn_branch(x, norm_pre, w_in_hgrn, lb_logits, norm_w, seq, layer):
    m, d = x.shape
    nk = HGRN_HEADS * HGRN_DK
    nv = HGRN_HEADS * HGRN_DV
    depth = lb_logits.shape[0]
    tm = _pick(seq, _HGRN_TM)
    assert HGRN_DK == _LANES and 2 * HGRN_DV == _LANES and tm % _HGRN_CHUNK == 0
    nw = jnp.tile(norm_w.reshape(1, HGRN_DV), (1, HGRN_HEADS))
    return pl.pallas_call(
        functools.partial(_hgrn_kernel, seq_tiles=seq // tm, layer=layer),
        grid=(m // tm,),
        in_specs=[
            pl.BlockSpec((tm, d), lambda i: (i, 0)),
            _const_spec((1, d)),
            _const_spec((d, 2 * nk + 2 * nv)),
            _const_spec((depth, nk)),
            _const_spec((1, nv)),
        ],
        out_specs=pl.BlockSpec((tm, nv), lambda i: (i, 0)),
        out_shape=jax.ShapeDtypeStruct((m, nv), _MXU_DTYPE),
        scratch_shapes=[
            pltpu.VMEM((tm, 2 * nk + 2 * nv), _F32),
            pltpu.VMEM((HGRN_HEADS, _LANES, HGRN_DK), _F32),
            pltpu.VMEM((HGRN_HEADS, _LANES, HGRN_DK), _F32),
        ],
        compiler_params=_params("arbitrary"),
        name="hgrn_branch",
    )(x, norm_pre.reshape(1, d), w_in_hgrn, lb_logits, nw)


def _merge_kernel(x_ref, npre_ref, wgc_ref, wga_ref, wgh_ref, yc_ref, ya_ref, yh_ref,
                  wbc_ref, wba_ref, wbh_ref, wo_ref, npost_ref, o_ref, h_ref):
    j = pl.program_id(1)
    last = pl.num_programs(1) - 1

    @pl.when(j == 0)
    def _():
        _prenorm_rows(x_ref, npre_ref, h_ref, o_ref)

    h = h_ref[...]
    merged = (jax.nn.sigmoid(_dot(h, wgc_ref[...])) * _dot(yc_ref[...], wbc_ref[...])
              + jax.nn.sigmoid(_dot(h, wga_ref[...])) * _dot(ya_ref[...], wba_ref[...])
              + jax.nn.sigmoid(_dot(h, wgh_ref[...])) * _dot(yh_ref[...], wbh_ref[...]))
    o_ref[...] += _dot(merged.astype(_MXU_DTYPE), wo_ref[...])

    @pl.when(j == last)
    def _():
        _residual_rows(x_ref, npost_ref, o_ref, 1.0)


def _merge_step(x, norm_pre, w_gates, y_conv, y_attn, y_hgrn, wb_conv, wb_attn, wb_hgrn, w_o, norm_post):
    m, d = x.shape
    tm, tn = _pick(m, _MIX_TM), _pick(d, _MIX_TN)
    nt = d // tn
    nc, na, nh = y_conv.shape[1], y_attn.shape[1], y_hgrn.shape[1]
    return pl.pallas_call(
        _merge_kernel,
        grid=(m // tm, nt),
        in_specs=[
            pl.BlockSpec((tm, d), lambda i, j: (i, 0)),
            pl.BlockSpec((1, d), lambda i, j: (0, 0)),
            pl.BlockSpec((d, tn), lambda i, j: (0, j)),
            pl.BlockSpec((d, tn), lambda i, j: (0, j + nt)),
            pl.BlockSpec((d, tn), lambda i, j: (0, j + 2 * nt)),
            pl.BlockSpec((tm, nc), lambda i, j: (i, 0)),
            pl.BlockSpec((tm, na), lambda i, j: (i, 0)),
            pl.BlockSpec((tm, nh), lambda i, j: (i, 0)),
            pl.BlockSpec((nc, tn), lambda i, j: (0, j)),
            pl.BlockSpec((na, tn), lambda i, j: (0, j)),
            pl.BlockSpec((nh, tn), lambda i, j: (0, j)),
            pl.BlockSpec((tn, d), lambda i, j: (j, 0)),
            pl.BlockSpec((1, d), lambda i, j: (0, 0)),
        ],
        out_specs=pl.BlockSpec((tm, d), lambda i, j: (i, 0)),
        out_shape=jax.ShapeDtypeStruct((m, d), _F32),
        scratch_shapes=[pltpu.VMEM((tm, d), _MXU_DTYPE)],
        compiler_params=_params("parallel", "arbitrary"),
        name="merge_step",
    )(x, norm_pre.reshape(1, d), w_gates, w_gates, w_gates, y_conv, y_attn, y_hgrn,
      wb_conv, wb_attn, wb_hgrn, w_o, norm_post.reshape(1, d))


def kernel(x, p, ffn1_norm_pre, ffn1_w_gu, ffn1_w_down, ffn1_norm_post, mix_norm_pre, w_in, conv_w, attn_sinks, hgrn_lb_logits, hgrn_norm, w_branch_conv, w_branch_attn, w_branch_hgrn, w_o, mix_norm_post, ffn2_norm_pre, ffn2_w_gu, ffn2_w_down, ffn2_norm_post, ple_norm_pre, w_ple_gate, w_ple_proj, ple_norm_post):
    batch, seq, d = x.shape
    depth = p.shape[0]
    m = batch * seq
    n_conv = 3 * CONV_WIDTH
    n_attn = (ATTN_HEADS + 2 * ATTN_KV_HEADS) * HEAD_DIM
    n_hgrn = 2 * HGRN_HEADS * (HGRN_DK + HGRN_DV)
    o_attn = n_conv
    o_hgrn = o_attn + n_attn
    o_gate = o_hgrn + n_hgrn

    def mx(w):
        return w.astype(_MXU_DTYPE)

    xf = x.reshape(m, d)
    for l in range(depth):
        xf = _ffn_half_step(xf, ffn1_norm_pre[l], mx(ffn1_w_gu[l]), mx(ffn1_w_down[l]), ffn1_norm_post[l])
        w = w_in[l]
        y_conv = _conv_branch(xf, mix_norm_pre[l], mx(w[:, :o_attn]), conv_w[l], seq)
        y_attn = _attn_branch(xf, mix_norm_pre[l], mx(w[:, o_attn:o_hgrn]), attn_sinks[l], seq)
        y_hgrn = _hgrn_branch(xf, mix_norm_pre[l], mx(w[:, o_hgrn:o_gate]), hgrn_lb_logits, hgrn_norm[l], seq, l)
        xf = _merge_step(xf, mix_norm_pre[l], mx(w[:, o_gate:]), y_conv, y_attn, y_hgrn,
                         mx(w_branch_conv[l]), mx(w_branch_attn[l]), mx(w_branch_hgrn[l]), mx(w_o[l]),
                         mix_norm_post[l])
        xf = _ffn_half_step(xf, ffn2_norm_pre[l], mx(ffn2_w_gu[l]), mx(ffn2_w_down[l]), ffn2_norm_post[l])
        xf = _ple_step(xf, p[l].reshape(m, -1), ple_norm_pre[l], mx(w_ple_gate[l]), mx(w_ple_proj[l]),
                       ple_norm_post[l])
    return xf.reshape(batch, seq, d)
```

```python
import functools
import math

import jax
import jax.numpy as jnp
from jax import lax
from jax.experimental import pallas as pl
from jax.experimental.pallas import tpu as pltpu

_MXU_DTYPE = jnp.bfloat16
_F32 = jnp.float32

NORM_EPS = 1e-6
NEG_BIG = -1e30
LB_FLOOR = 1e-20

CONV_WIDTH = 512
CONV_K = 3
ATTN_HEADS = 16
ATTN_KV_HEADS = 2
HEAD_DIM = 64
ATTN_BLOCK = 128
HGRN_HEADS = 8
HGRN_DK = 128
HGRN_DV = 64

_LANES = 128
_HALF = 64

_HGRN_CHUNK = 64
_HGRN_SAFE_LOG_DECAY = 80.0

_FFN_TM = 1024
_FFN_RB = 512
_NORM_ROWS = 128
_FFN_TF = 512
_MIX_TM = 512
_MIX_TN = 512
_PLE_TM = 1024
_CONV_TM = 512
_ROW_BLOCK = 256
_CAST_ROWS = 512
_CAST_COLS = 2816
_ATTN_TM = 512
_HGRN_TM = 512
_HGRN_UNROLL = 2
_ATTN_SKEW = 6

_VMEM_LIMIT = 60 * 1024 * 1024


def _pick(n, pref):
    t = min(n, pref)
    assert n % t == 0, (n, pref)
    return t


def _rmsnorm(xf, w):
    return xf * lax.rsqrt(jnp.mean(xf * xf, axis=-1, keepdims=True) + NORM_EPS) * w


def _rows_loop(n_rows, body):
    def step(r, carry):
        body(pl.ds(pl.multiple_of(r * _NORM_ROWS, _NORM_ROWS), _NORM_ROWS))
        return carry

    lax.fori_loop(0, n_rows // _NORM_ROWS, step, 0, unroll=2)


def _prenorm_rows(x_ref, w_ref, h_ref, o_ref):
    w = w_ref[...]

    def body(rows):
        h_ref[rows, :] = _rmsnorm(x_ref[rows, :], w).astype(h_ref.dtype)
        o_ref[rows, :] = jnp.zeros((_NORM_ROWS, o_ref.shape[1]), o_ref.dtype)

    _rows_loop(x_ref.shape[0], body)


def _residual_rows(x_ref, w_ref, o_ref, scale):
    w = w_ref[...] * scale

    def body(rows):
        o_ref[rows, :] = x_ref[rows, :] + _rmsnorm(o_ref[rows, :], w)

    _rows_loop(x_ref.shape[0], body)


def _dot(a, b):
    return jnp.dot(a, b, preferred_element_type=_F32)


def _dot_nt(a, b):
    return lax.dot_general(a, b, (((1,), (1,)), ((), ())), preferred_element_type=_F32)


def _dot_tn(a, b):
    return lax.dot_general(a, b, (((0,), (0,)), ((), ())), preferred_element_type=_F32)


def _params(*sem):
    return pltpu.CompilerParams(dimension_semantics=sem, vmem_limit_bytes=_VMEM_LIMIT)


def _const_spec(shape):
    nd = len(shape)
    return pl.BlockSpec(shape, lambda *_: (0,) * nd, pipeline_mode=pl.Buffered(1))


def _cast_kernel(w_ref, o_ref):
    o_ref[...] = w_ref[...].astype(o_ref.dtype)


def _to_mxu(w, layer, col0=0, width=None):
    _, r, c = w.shape
    width = c if width is None else width
    span = math.gcd(col0, width)
    tc = max(t for t in range(_LANES, min(span, _CAST_COLS) + 1, _LANES) if span % t == 0)
    tr = _pick(r, _CAST_ROWS)
    return pl.pallas_call(
        _cast_kernel,
        grid=(r // tr, width // tc),
        in_specs=[pl.BlockSpec((None, tr, tc), lambda i, j: (layer, i, j + col0 // tc))],
        out_specs=pl.BlockSpec((tr, tc), lambda i, j: (i, j)),
        out_shape=jax.ShapeDtypeStruct((r, width), _MXU_DTYPE),
        compiler_params=_params("parallel", "parallel"),
        name="weight_cast",
    )(w)


def _ffn_kernel(x_ref, npre_ref, wg_ref, wu_ref, wd_ref, npost_ref, o_ref, h_ref, *, rb):
    j = pl.program_id(1)
    last = pl.num_programs(1) - 1
    tm = x_ref.shape[0]

    @pl.when(j == 0)
    def _():
        _prenorm_rows(x_ref, npre_ref, h_ref, o_ref)

    for r in range(tm // rb):
        rows = slice(r * rb, (r + 1) * rb)
        h = h_ref[rows, :]
        g = _dot(h, wg_ref[...])
        u = _dot(h, wu_ref[...])
        a = (g * jax.nn.sigmoid(g) * u).astype(_MXU_DTYPE)
        o_ref[rows, :] += _dot(a, wd_ref[...])

    @pl.when(j == last)
    def _():
        _residual_rows(x_ref, npost_ref, o_ref, 0.5)


def _ffn_half_step(x, norm_pre, w_gu, w_down, norm_post):
    m, d = x.shape
    f = w_down.shape[0]
    tm, tf = _pick(m, _FFN_TM), _pick(f, _FFN_TF)
    nf = f // tf
    return pl.pallas_call(
        functools.partial(_ffn_kernel, rb=min(tm, _FFN_RB)),
        grid=(m // tm, nf),
        in_specs=[
            pl.BlockSpec((tm, d), lambda i, j: (i, 0)),
            pl.BlockSpec((1, d), lambda i, j: (0, 0)),
            pl.BlockSpec((d, tf), lambda i, j: (0, j)),
            pl.BlockSpec((d, tf), lambda i, j: (0, j + nf)),
            pl.BlockSpec((tf, d), lambda i, j: (j, 0)),
            pl.BlockSpec((1, d), lambda i, j: (0, 0)),
        ],
        out_specs=pl.BlockSpec((tm, d), lambda i, j: (i, 0)),
        out_shape=jax.ShapeDtypeStruct((m, d), _F32),
        scratch_shapes=[pltpu.VMEM((tm, d), _MXU_DTYPE)],
        compiler_params=_params("parallel", "arbitrary"),
        name="ffn_half_step",
    )(x, norm_pre.reshape(1, d), w_gu, w_gu, w_down, norm_post.reshape(1, d))


def _ple_kernel(x_ref, p_ref, npre_ref, wg_ref, wp_ref, npost_ref, o_ref, *, rb):
    for r in range(x_ref.shape[0] // rb):
        rows = slice(r * rb, (r + 1) * rb)
        x = x_ref[rows, :]
        hp = _rmsnorm(x, npre_ref[...]).astype(_MXU_DTYPE)
        gate = jax.nn.sigmoid(_dot(hp, wg_ref[...]))
        proj = _dot(p_ref[rows, :].astype(_MXU_DTYPE), wp_ref[...])
        o_ref[rows, :] = x + _rmsnorm(gate * proj, npost_ref[...])


def _ple_step(x, p, layer, norm_pre, w_gate, w_proj, norm_post):
    m, d = x.shape
    dp = p.shape[2]
    tm = _pick(m, _PLE_TM)
    return pl.pallas_call(
        functools.partial(_ple_kernel, rb=min(tm, _ROW_BLOCK)),
        grid=(m // tm,),
        in_specs=[
            pl.BlockSpec((tm, d), lambda i: (i, 0)),
            pl.BlockSpec((None, tm, dp), lambda i: (layer, i, 0)),
            _const_spec((1, d)),
            _const_spec((d, d)),
            _const_spec((dp, d)),
            _const_spec((1, d)),
        ],
        out_specs=pl.BlockSpec((tm, d), lambda i: (i, 0)),
        out_shape=jax.ShapeDtypeStruct((m, d), _F32),
        compiler_params=_params("parallel"),
        name="ple_step",
    )(x, p, norm_pre.reshape(1, d), w_gate, w_proj, norm_post.reshape(1, d))


def _conv_kernel(x_ref, npre_ref, w_ref, cw_ref, o_ref, carry_ref, *, seq_tiles, rb):
    first = (pl.program_id(0) % seq_tiles) == 0
    tm = x_ref.shape[0]
    cwid = o_ref.shape[1]

    @pl.when(first)
    def _():
        carry_ref[...] = jnp.zeros_like(carry_ref)

    cw = cw_ref[...]
    row = lax.broadcasted_iota(jnp.int32, (rb, cwid), 0)
    tail = carry_ref[...]
    for r in range(tm // rb):
        rows = slice(r * rb, (r + 1) * rb)
        h = _rmsnorm(x_ref[rows, :], npre_ref[...]).astype(_MXU_DTYPE)
        proj = _dot(h, w_ref[...])
        xin, b_gate, c_gate = proj[:, :cwid], proj[:, cwid:2 * cwid], proj[:, 2 * cwid:]
        u = c_gate * xin
        prev1 = tail[7:8, :]
        prev2 = tail[6:7, :]
        u1 = jnp.where(row == 0, prev1, pltpu.roll(u, 1, 0))
        u2 = jnp.where(row == 0, prev2, jnp.where(row == 1, prev1, pltpu.roll(u, 2, 0)))
        y = cw[2:3, :] * u + cw[1:2, :] * u1 + cw[0:1, :] * u2
        o_ref[rows, :] = (b_gate * y).astype(o_ref.dtype)
        tail = u[rb - 8:, :]
    carry_ref[...] = tail


def _conv_branch(x, norm_pre, w_in_conv, conv_w, seq):
    m, d = x.shape
    cwid = conv_w.shape[1]
    tm = _pick(seq, _CONV_TM)
    cw = jnp.zeros((8, cwid), _F32).at[:CONV_K].set(conv_w)
    return pl.pallas_call(
        functools.partial(_conv_kernel, seq_tiles=seq // tm, rb=min(tm, _ROW_BLOCK)),
        grid=(m // tm,),
        in_specs=[
            pl.BlockSpec((tm, d), lambda i: (i, 0)),
            _const_spec((1, d)),
            _const_spec((d, 3 * cwid)),
            _const_spec((8, cwid)),
        ],
        out_specs=pl.BlockSpec((tm, cwid), lambda i: (i, 0)),
        out_shape=jax.ShapeDtypeStruct((m, cwid), _MXU_DTYPE),
        scratch_shapes=[pltpu.VMEM((8, cwid), _F32)],
        compiler_params=_params("arbitrary"),
        name="conv_branch",
    )(x, norm_pre.reshape(1, d), w_in_conv, cw)


def _attn_kernel(sink_ref, x_ref, npre_ref, w_ref, o_ref, q_s, k_s, v_s, bias_s, *, seq_tiles):
    blk = ATTN_BLOCK
    tm = x_ref.shape[0]
    nq = ATTN_HEADS * HEAD_DIM
    nkv = ATTN_KV_HEADS * HEAD_DIM
    first = (pl.program_id(0) % seq_tiles) == 0
    group = ATTN_HEADS // ATTN_KV_HEADS

    @pl.when(pl.program_id(0) == 0)
    def _():
        qi = lax.broadcasted_iota(jnp.int32, (blk, 2 * blk), 0)
        ki = lax.broadcasted_iota(jnp.int32, (blk, 2 * blk), 1)
        dist = qi + blk - ki
        band = (dist >= 0) & (dist < blk)
        distf = dist.astype(_F32)
        for head in range(ATTN_HEADS):
            slope = 2.0 ** (-8.0 * (head + 1) / ATTN_HEADS)
            bias_s[0, head] = jnp.where(band, -slope * distf, NEG_BIG)
            bias_s[1, head] = jnp.where(band & (ki >= blk), -slope * distf, NEG_BIG)

    h = _rmsnorm(x_ref[...], npre_ref[...]).astype(_MXU_DTYPE)
    qkv = _dot(h, w_ref[...])
    q_s[...] = (qkv[:, :nq] * (HEAD_DIM ** -0.5)).astype(q_s.dtype)

    @pl.when(first)
    def _():
        k_s[0:blk, :] = jnp.zeros((blk, nkv), _F32)
        v_s[0:blk, :] = jnp.zeros((blk, nkv), _F32)

    k_s[blk:blk + tm, :] = qkv[:, nq:nq + nkv]
    v_s[blk:blk + tm, :] = qkv[:, nq + nkv:nq + 2 * nkv]

    lo = lax.broadcasted_iota(jnp.int32, (1, _LANES), 1) < _HALF

    def block(b, carry):
        r0 = pl.multiple_of(b * blk, blk)
        kc = k_s[pl.ds(r0, 2 * blk), :]
        vc = v_s[pl.ds(r0, 2 * blk), :]
        kr = pltpu.roll(kc, _HALF, 1)
        vr = pltpu.roll(vc, _HALF, 1)
        k_dup = [jnp.where(lo, kc, kr).astype(_MXU_DTYPE), jnp.where(lo, kr, kc).astype(_MXU_DTYPE)]
        v_half = [
            [jnp.where(lo, vc, 0.0).astype(_MXU_DTYPE), jnp.where(lo, 0.0, vr).astype(_MXU_DTYPE)],
            [jnp.where(lo, vr, 0.0).astype(_MXU_DTYPE), jnp.where(lo, 0.0, vc).astype(_MXU_DTYPE)],
        ]
        no_prev = jnp.logical_and(first, b == 0).astype(jnp.int32)
        def scores(head):
            pair, half = divmod(head, 2)
            qp = q_s[pl.ds(r0, blk), pair * _LANES:(pair + 1) * _LANES]
            zero = jnp.zeros_like(qp)
            qm = jnp.where(lo, qp, zero) if half == 0 else jnp.where(lo, zero, qp)
            s = _dot_nt(qm, k_dup[head // group]) + bias_s[no_prev, head]
            sink = sink_ref[head]
            mx = jnp.maximum(jnp.max(s, axis=-1, keepdims=True), sink)
            pr = jnp.exp(s - mx)
            denom = jnp.sum(pr, axis=-1, keepdims=True) + jnp.exp(sink - mx)
            return pr.astype(_MXU_DTYPE), denom

        acc = {}

        def values(head, pr, denom):
            pair, half = divmod(head, 2)
            o = _dot(pr, v_half[head // group][half]) / denom
            if half == 0:
                acc[pair] = o
            else:
                o_ref[pl.ds(r0, blk), pair * _LANES:(pair + 1) * _LANES] = (acc.pop(pair) + o).astype(o_ref.dtype)

        pending = []
        for head in range(ATTN_HEADS):
            pending.append((head,) + scores(head))
            if len(pending) > _ATTN_SKEW:
                values(*pending.pop(0))
        while pending:
            values(*pending.pop(0))
        return carry

    lax.fori_loop(0, tm // blk, block, 0)
    k_s[0:blk, :] = k_s[tm:tm + blk, :]
    v_s[0:blk, :] = v_s[tm:tm + blk, :]


def _attn_branch(x, norm_pre, w_in_attn, sinks, seq):
    m, d = x.shape
    nq = ATTN_HEADS * HEAD_DIM
    nkv = ATTN_KV_HEADS * HEAD_DIM
    tm = _pick(seq, _ATTN_TM)
    assert tm % ATTN_BLOCK == 0 and nkv == _LANES
    return pl.pallas_call(
        functools.partial(_attn_kernel, seq_tiles=seq // tm),
        grid=(m // tm,),
        in_specs=[
            pl.BlockSpec(memory_space=pltpu.SMEM),
            pl.BlockSpec((tm, d), lambda i: (i, 0)),
            _const_spec((1, d)),
            _const_spec((d, nq + 2 * nkv)),
        ],
        out_specs=pl.BlockSpec((tm, nq), lambda i: (i, 0)),
        out_shape=jax.ShapeDtypeStruct((m, nq), _MXU_DTYPE),
        scratch_shapes=[
            pltpu.VMEM((tm, nq), _MXU_DTYPE),
            pltpu.VMEM((tm + ATTN_BLOCK, nkv), _F32),
            pltpu.VMEM((tm + ATTN_BLOCK, nkv), _F32),
            pltpu.VMEM((2, ATTN_HEADS, ATTN_BLOCK, 2 * ATTN_BLOCK), _F32),
        ],
        compiler_params=_params("arbitrary"),
        name="attn_branch",
    )(sinks, x, norm_pre.reshape(1, d), w_in_attn)


def _hgrn_kernel(x_ref, npre_ref, w_ref, lbl_ref, nw_ref, o_ref, pr_s, st_s, bak_s, *, seq_tiles, layer):
    ch = _HGRN_CHUNK
    tm = x_ref.shape[0]
    nk = HGRN_HEADS * HGRN_DK
    nv = HGRN_HEADS * HGRN_DV
    first = (pl.program_id(0) % seq_tiles) == 0

    @pl.when(first)
    def _():
        st_s[...] = jnp.zeros_like(st_s)

    h = _rmsnorm(x_ref[...], npre_ref[...]).astype(_MXU_DTYPE)
    pr_s[...] = _dot(h, w_ref[...])

    lg = lbl_ref[...]
    e = jnp.exp(lg - jnp.max(lg, axis=0, keepdims=True))
    sm = e / jnp.sum(e, axis=0, keepdims=True)
    lb = jnp.zeros((1, nk), _F32)
    for prev in range(layer):
        lb = lb + sm[prev:prev + 1, :]
    lb_floor = jnp.maximum(lb, LB_FLOOR)
    one_minus_lb = 1.0 - lb

    lo = lax.broadcasted_iota(jnp.int32, (1, _LANES), 1) < _HALF
    rid = lax.broadcasted_iota(jnp.int32, (ch, nk), 0)
    tril = lax.broadcasted_iota(jnp.int32, (ch, ch), 0) >= lax.broadcasted_iota(jnp.int32, (ch, ch), 1)
    nw = nw_ref[...]

    def chunk(c, a_min, *, exact):
        rows = pl.ds(pl.multiple_of(c * ch, ch), ch)
        q = pr_s[rows, 0:nk]
        z = pr_s[rows, nk:2 * nk]
        qs = q * jax.nn.sigmoid(q)
        sig_pos = jax.nn.sigmoid(z)
        sig_neg = 1.0 - sig_pos
        log_f = jnp.log(sig_pos + lb_floor * sig_neg)
        kk = one_minus_lb * sig_neg
        a = log_f
        step = 1
        while step < ch:
            a = a + jnp.where(rid >= step, pltpu.roll(a, step, 0), 0.0)
            step *= 2
        a_last = a[ch - 1:ch, :]
        q_dec = (qs * jnp.exp(a)).astype(_MXU_DTYPE)
        k_end = (kk * jnp.exp(a_last - a)).astype(_MXU_DTYPE)
        d_end = jnp.exp(a_last)
        heads = range(HGRN_HEADS)
        sls = [slice(hd * HGRN_DK, (hd + 1) * HGRN_DK) for hd in heads]
        vps = [pr_s[rows, 2 * nk + pr * _LANES:2 * nk + (pr + 1) * _LANES] for pr in range(HGRN_HEADS // 2)]
        vpm = [v.astype(_MXU_DTYPE) for v in vps]
        if exact:
            sid = lax.broadcasted_iota(jnp.int32, (ch, nk), 0)
            sid_v = lax.broadcasted_iota(jnp.int32, (ch, _LANES), 0)

            def query_row(t, intra):
                a_t = jnp.sum(jnp.where(sid == t, a, 0.0), axis=0, keepdims=True)
                q_t = jnp.sum(jnp.where(sid == t, qs, 0.0), axis=0, keepdims=True)
                wts = q_t * jnp.exp(jnp.where(sid <= t, a_t - a, NEG_BIG)) * kk
                out = []
                for hd in heads:
                    col = jnp.sum(wts[:, sls[hd]], axis=-1, keepdims=True)
                    o_t = jnp.sum(col * vps[hd // 2], axis=0, keepdims=True)
                    out.append(jnp.where(sid_v == t, o_t, intra[hd]))
                return tuple(out)

            intra = lax.fori_loop(0, ch, query_row, tuple(jnp.zeros((ch, _LANES), _F32) for _ in heads))
        else:
            k_inv = (kk * jnp.exp(-a)).astype(_MXU_DTYPE)
            att = [_dot_nt(q_dec[:, sls[hd]], k_inv[:, sls[hd]]) for hd in heads]
        upd = [_dot_tn(vpm[hd // 2], k_end[:, sls[hd]]) for hd in heads]
        states = [st_s[hd] for hd in heads]
        inter = [_dot_nt(q_dec[:, sls[hd]], states[hd].astype(_MXU_DTYPE)) for hd in heads]
        if not exact:
            intra = [_dot(jnp.where(tril, att[hd], 0.0).astype(_MXU_DTYPE), vpm[hd // 2]) for hd in heads]
        for hd in heads:
            st_s[hd] = states[hd] * d_end[:, sls[hd]] + upd[hd]
        for pair in range(HGRN_HEADS // 2):
            gp = pr_s[rows, 2 * nk + nv + pair * _LANES:2 * nk + nv + (pair + 1) * _LANES]
            outs = [inter[2 * pair + half] + intra[2 * pair + half] for half in range(2)]
            o = jnp.where(lo, outs[0], outs[1])
            osq = o * o
            ms_lo = jnp.sum(jnp.where(lo, osq, 0.0), axis=-1, keepdims=True) * (1.0 / HGRN_DV)
            ms_hi = jnp.sum(jnp.where(lo, 0.0, osq), axis=-1, keepdims=True) * (1.0 / HGRN_DV)
            rinv = jnp.where(lo, lax.rsqrt(ms_lo + NORM_EPS), lax.rsqrt(ms_hi + NORM_EPS))
            on = o * rinv * nw[:, pair * _LANES:(pair + 1) * _LANES]
            o_ref[rows, pair * _LANES:(pair + 1) * _LANES] = (on * (gp * jax.nn.sigmoid(gp))).astype(o_ref.dtype)
        return jnp.minimum(a_min, a_last)

    bak_s[...] = st_s[...]
    a_min = lax.fori_loop(0, tm // ch, functools.partial(chunk, exact=False), jnp.zeros((1, nk), _F32),
                          unroll=_HGRN_UNROLL)

    @pl.when(jnp.min(a_min) < -_HGRN_SAFE_LOG_DECAY)
    def _():
        st_s[...] = bak_s[...]
        lax.fori_loop(0, tm // ch, functools.partial(chunk, exact=True), jnp.zeros((1, nk), _F32))


def _hgrn_branch(x, norm_pre, w_in_hgrn, lb_logits, norm_w, seq, layer):
    m, d = x.shape
    nk = HGRN_HEADS * HGRN_DK
    nv = HGRN_HEADS * HGRN_DV
    depth = lb_logits.shape[0]
    tm = _pick(seq, _HGRN_TM)
    assert HGRN_DK == _LANES and 2 * HGRN_DV == _LANES and tm % _HGRN_CHUNK == 0
    nw = jnp.tile(norm_w.reshape(1, HGRN_DV), (1, HGRN_HEADS))
    return pl.pallas_call(
        functools.partial(_hgrn_kernel, seq_tiles=seq // tm, layer=layer),
        grid=(m // tm,),
        in_specs=[
            pl.BlockSpec((tm, d), lambda i: (i, 0)),
            _const_spec((1, d)),
            _const_spec((d, 2 * nk + 2 * nv)),
            _const_spec((depth, nk)),
            _const_spec((1, nv)),
        ],
        out_specs=pl.BlockSpec((tm, nv), lambda i: (i, 0)),
        out_shape=jax.ShapeDtypeStruct((m, nv), _MXU_DTYPE),
        scratch_shapes=[
            pltpu.VMEM((tm, 2 * nk + 2 * nv), _F32),
            pltpu.VMEM((HGRN_HEADS, _LANES, HGRN_DK), _F32),
            pltpu.VMEM((HGRN_HEADS, _LANES, HGRN_DK), _F32),
        ],
        compiler_params=_params("arbitrary"),
        name="hgrn_branch",
    )(x, norm_pre.reshape(1, d), w_in_hgrn, lb_logits, nw)


def _merge_kernel(x_ref, npre_ref, wgc_ref, wga_ref, wgh_ref, yc_ref, ya_ref, yh_ref,
                  wbc_ref, wba_ref, wbh_ref, wo_ref, npost_ref, o_ref, h_ref, *, rb):
    j = pl.program_id(1)
    last = pl.num_programs(1) - 1

    @pl.when(j == 0)
    def _():
        _prenorm_rows(x_ref, npre_ref, h_ref, o_ref)

    for r in range(x_ref.shape[0] // rb):
        rows = slice(r * rb, (r + 1) * rb)
        h = h_ref[rows, :]
        merged = (jax.nn.sigmoid(_dot(h, wgc_ref[...])) * _dot(yc_ref[rows, :], wbc_ref[...])
                  + jax.nn.sigmoid(_dot(h, wga_ref[...])) * _dot(ya_ref[rows, :], wba_ref[...])
                  + jax.nn.sigmoid(_dot(h, wgh_ref[...])) * _dot(yh_ref[rows, :], wbh_ref[...]))
        o_ref[rows, :] += _dot(merged.astype(_MXU_DTYPE), wo_ref[...])

    @pl.when(j == last)
    def _():
        _residual_rows(x_ref, npost_ref, o_ref, 1.0)


def _merge_step(x, norm_pre, w_gates, y_conv, y_attn, y_hgrn, wb_conv, wb_attn, wb_hgrn, w_o, norm_post):
    m, d = x.shape
    tm, tn = _pick(m, _MIX_TM), _pick(d, _MIX_TN)
    nt = d // tn
    nc, na, nh = y_conv.shape[1], y_attn.shape[1], y_hgrn.shape[1]
    return pl.pallas_call(
        functools.partial(_merge_kernel, rb=min(tm, _FFN_RB)),
        grid=(m // tm, nt),
        in_specs=[
            pl.BlockSpec((tm, d), lambda i, j: (i, 0)),
            pl.BlockSpec((1, d), lambda i, j: (0, 0)),
            pl.BlockSpec((d, tn), lambda i, j: (0, j)),
            pl.BlockSpec((d, tn), lambda i, j: (0, j + nt)),
            pl.BlockSpec((d, tn), lambda i, j: (0, j + 2 * nt)),
            pl.BlockSpec((tm, nc), lambda i, j: (i, 0)),
            pl.BlockSpec((tm, na), lambda i, j: (i, 0)),
            pl.BlockSpec((tm, nh), lambda i, j: (i, 0)),
            pl.BlockSpec((nc, tn), lambda i, j: (0, j)),
            pl.BlockSpec((na, tn), lambda i, j: (0, j)),
            pl.BlockSpec((nh, tn), lambda i, j: (0, j)),
            pl.BlockSpec((tn, d), lambda i, j: (j, 0)),
            pl.BlockSpec((1, d), lambda i, j: (0, 0)),
        ],
        out_specs=pl.BlockSpec((tm, d), lambda i, j: (i, 0)),
        out_shape=jax.ShapeDtypeStruct((m, d), _F32),
        scratch_shapes=[pltpu.VMEM((tm, d), _MXU_DTYPE)],
        compiler_params=_params("parallel", "arbitrary"),
        name="merge_step",
    )(x, norm_pre.reshape(1, d), w_gates, w_gates, w_gates, y_conv, y_attn, y_hgrn,
      wb_conv, wb_attn, wb_hgrn, w_o, norm_post.reshape(1, d))


def kernel(x, p, ffn1_norm_pre, ffn1_w_gu, ffn1_w_down, ffn1_norm_post, mix_norm_pre, w_in, conv_w, attn_sinks, hgrn_lb_logits, hgrn_norm, w_branch_conv, w_branch_attn, w_branch_hgrn, w_o, mix_norm_post, ffn2_norm_pre, ffn2_w_gu, ffn2_w_down, ffn2_norm_post, ple_norm_pre, w_ple_gate, w_ple_proj, ple_norm_post):
    batch, seq, d = x.shape
    depth = p.shape[0]
    m = batch * seq
    n_conv = 3 * CONV_WIDTH
    n_attn = (ATTN_HEADS + 2 * ATTN_KV_HEADS) * HEAD_DIM
    n_hgrn = 2 * HGRN_HEADS * (HGRN_DK + HGRN_DV)
    o_attn = n_conv
    o_hgrn = o_attn + n_attn
    o_gate = o_hgrn + n_hgrn

    xf = x.reshape(m, d)
    for l in range(depth):
        xf = _ffn_half_step(xf, ffn1_norm_pre[l], _to_mxu(ffn1_w_gu, l), _to_mxu(ffn1_w_down, l), ffn1_norm_post[l])
        y_conv = _conv_branch(xf, mix_norm_pre[l], _to_mxu(w_in, l, 0, n_conv), conv_w[l], seq)
        y_attn = _attn_branch(xf, mix_norm_pre[l], _to_mxu(w_in, l, o_attn, n_attn), attn_sinks[l], seq)
        y_hgrn = _hgrn_branch(xf, mix_norm_pre[l], _to_mxu(w_in, l, o_hgrn, n_hgrn), hgrn_lb_logits, hgrn_norm[l],
                              seq, l)
        xf = _merge_step(xf, mix_norm_pre[l], _to_mxu(w_in, l, o_gate, 3 * d), y_conv, y_attn, y_hgrn,
                         _to_mxu(w_branch_conv, l), _to_mxu(w_branch_attn, l), _to_mxu(w_branch_hgrn, l),
                         _to_mxu(w_o, l), mix_norm_post[l])
        xf = _ffn_half_step(xf, ffn2_norm_pre[l], _to_mxu(ffn2_w_gu, l), _to_mxu(ffn2_w_down, l), ffn2_norm_post[l])
        xf = _ple_step(xf, p.reshape(depth, m, -1), l, ple_norm_pre[l], _to_mxu(w_ple_gate, l),
                       _to_mxu(w_ple_proj, l), ple_norm_post[l])
    return xf.reshape(batch, seq, d)
```

```python
import functools
import math

import jax
import jax.numpy as jnp
from jax import lax
from jax.experimental import pallas as pl
from jax.experimental.pallas import tpu as pltpu

_MXU_DTYPE = jnp.bfloat16
_F32 = jnp.float32

NORM_EPS = 1e-6
NEG_BIG = -1e30
LB_FLOOR = 1e-20

CONV_WIDTH = 512
CONV_K = 3
ATTN_HEADS = 16
ATTN_KV_HEADS = 2
HEAD_DIM = 64
ATTN_BLOCK = 128
HGRN_HEADS = 8
HGRN_DK = 128
HGRN_DV = 64

_LANES = 128
_HALF = 64

_HGRN_CHUNK = 64
_HGRN_SAFE_LOG_DECAY = 80.0

_FFN_TM = 1024
_FFN_RB = 512
_NORM_ROWS = 128
_FFN_TF = 512
_MIX_TM = 512
_MIX_TN = 512
_PLE_TM = 1024
_CONV_TM = 512
_ROW_BLOCK = 256
_CAST_ROWS = 512
_CAST_COLS = 2816
_SPLIT_ROWS = 128
_ATTN_TM = 512
_HGRN_TM = 512
_HGRN_UNROLL = 2
_ATTN_SKEW = 6

_VMEM_LIMIT = 60 * 1024 * 1024


def _pick(n, pref):
    t = min(n, pref)
    assert n % t == 0, (n, pref)
    return t


def _rmsnorm(xf, w):
    return xf * lax.rsqrt(jnp.mean(xf * xf, axis=-1, keepdims=True) + NORM_EPS) * w


def _rows_loop(n_rows, body):
    def step(r, carry):
        body(pl.ds(pl.multiple_of(r * _NORM_ROWS, _NORM_ROWS), _NORM_ROWS))
        return carry

    lax.fori_loop(0, n_rows // _NORM_ROWS, step, 0, unroll=2)


def _prenorm_rows(x_ref, w_ref, h_ref, o_ref):
    w = w_ref[...]

    def body(rows):
        h_ref[rows, :] = _rmsnorm(x_ref[rows, :], w).astype(h_ref.dtype)
        o_ref[rows, :] = jnp.zeros((_NORM_ROWS, o_ref.shape[1]), o_ref.dtype)

    _rows_loop(x_ref.shape[0], body)


def _residual_rows(x_ref, w_ref, o_ref, scale):
    w = w_ref[...] * scale

    def body(rows):
        o_ref[rows, :] = x_ref[rows, :] + _rmsnorm(o_ref[rows, :], w)

    _rows_loop(x_ref.shape[0], body)


def _dot(a, b):
    return jnp.dot(a, b, preferred_element_type=_F32)


def _dot_nt(a, b):
    return lax.dot_general(a, b, (((1,), (1,)), ((), ())), preferred_element_type=_F32)


def _dot_tn(a, b):
    return lax.dot_general(a, b, (((0,), (0,)), ((), ())), preferred_element_type=_F32)


def _params(*sem):
    return pltpu.CompilerParams(dimension_semantics=sem, vmem_limit_bytes=_VMEM_LIMIT)


def _const_spec(shape):
    nd = len(shape)
    return pl.BlockSpec(shape, lambda *_: (0,) * nd, pipeline_mode=pl.Buffered(1))


def _cast_kernel(w_ref, o_ref):
    o_ref[...] = w_ref[...].astype(o_ref.dtype)


def _to_mxu(w, layer, col0=0, width=None):
    _, r, c = w.shape
    width = c if width is None else width
    span = math.gcd(col0, width)
    tc = max(t for t in range(_LANES, min(span, _CAST_COLS) + 1, _LANES) if span % t == 0)
    tr = _pick(r, _CAST_ROWS)
    return pl.pallas_call(
        _cast_kernel,
        grid=(r // tr, width // tc),
        in_specs=[pl.BlockSpec((None, tr, tc), lambda i, j: (layer, i, j + col0 // tc))],
        out_specs=pl.BlockSpec((tr, tc), lambda i, j: (i, j)),
        out_shape=jax.ShapeDtypeStruct((r, width), _MXU_DTYPE),
        compiler_params=_params("parallel", "parallel"),
        name="weight_cast",
    )(w)


def _split_cast_kernel(w_ref, *o_refs):
    col = 0
    for o_ref in o_refs:
        o_ref[...] = w_ref[:, col:col + o_ref.shape[1]].astype(o_ref.dtype)
        col += o_ref.shape[1]


def _split_to_mxu(w, layer, widths):
    _, r, c = w.shape
    assert sum(widths) == c and all(n % _LANES == 0 for n in widths)
    tr = _pick(r, _SPLIT_ROWS)
    return pl.pallas_call(
        _split_cast_kernel,
        grid=(r // tr,),
        in_specs=[pl.BlockSpec((None, tr, c), lambda i: (layer, i, 0))],
        out_specs=[pl.BlockSpec((tr, n), lambda i: (i, 0)) for n in widths],
        out_shape=[jax.ShapeDtypeStruct((r, n), _MXU_DTYPE) for n in widths],
        compiler_params=_params("parallel"),
        name="weight_split_cast",
    )(w)


def _ffn_kernel(x_ref, npre_ref, wg_ref, wu_ref, wd_ref, npost_ref, o_ref, h_ref, *, rb):
    j = pl.program_id(1)
    last = pl.num_programs(1) - 1
    tm = x_ref.shape[0]

    @pl.when(j == 0)
    def _():
        _prenorm_rows(x_ref, npre_ref, h_ref, o_ref)

    for r in range(tm // rb):
        rows = slice(r * rb, (r + 1) * rb)
        h = h_ref[rows, :]
        g = _dot(h, wg_ref[...])
        u = _dot(h, wu_ref[...])
        a = (g * jax.nn.sigmoid(g) * u).astype(_MXU_DTYPE)
        o_ref[rows, :] += _dot(a, wd_ref[...])

    @pl.when(j == last)
    def _():
        _residual_rows(x_ref, npost_ref, o_ref, 0.5)


def _ffn_half_step(x, norm_pre, w_gu, w_down, norm_post):
    m, d = x.shape
    f = w_down.shape[0]
    tm, tf = _pick(m, _FFN_TM), _pick(f, _FFN_TF)
    nf = f // tf
    return pl.pallas_call(
        functools.partial(_ffn_kernel, rb=min(tm, _FFN_RB)),
        grid=(m // tm, nf),
        in_specs=[
            pl.BlockSpec((tm, d), lambda i, j: (i, 0)),
            pl.BlockSpec((1, d), lambda i, j: (0, 0)),
            pl.BlockSpec((d, tf), lambda i, j: (0, j)),
            pl.BlockSpec((d, tf), lambda i, j: (0, j + nf)),
            pl.BlockSpec((tf, d), lambda i, j: (j, 0)),
            pl.BlockSpec((1, d), lambda i, j: (0, 0)),
        ],
        out_specs=pl.BlockSpec((tm, d), lambda i, j: (i, 0)),
        out_shape=jax.ShapeDtypeStruct((m, d), _F32),
        scratch_shapes=[pltpu.VMEM((tm, d), _MXU_DTYPE)],
        compiler_params=_params("parallel", "arbitrary"),
        name="ffn_half_step",
    )(x, norm_pre.reshape(1, d), w_gu, w_gu, w_down, norm_post.reshape(1, d))


def _ple_kernel(x_ref, p_ref, npre_ref, wg_ref, wp_ref, npost_ref, o_ref, *, rb):
    for r in range(x_ref.shape[0] // rb):
        rows = slice(r * rb, (r + 1) * rb)
        x = x_ref[rows, :]
        hp = _rmsnorm(x, npre_ref[...]).astype(_MXU_DTYPE)
        gate = jax.nn.sigmoid(_dot(hp, wg_ref[...]))
        proj = _dot(p_ref[rows, :].astype(_MXU_DTYPE), wp_ref[...])
        o_ref[rows, :] = x + _rmsnorm(gate * proj, npost_ref[...])


def _ple_step(x, p, layer, norm_pre, w_gate, w_proj, norm_post):
    m, d = x.shape
    dp = p.shape[2]
    tm = _pick(m, _PLE_TM)
    return pl.pallas_call(
        functools.partial(_ple_kernel, rb=min(tm, _FFN_RB)),
        grid=(m // tm,),
        in_specs=[
            pl.BlockSpec((tm, d), lambda i: (i, 0)),
            pl.BlockSpec((None, tm, dp), lambda i: (layer, i, 0)),
            _const_spec((1, d)),
            _const_spec((d, d)),
            _const_spec((dp, d)),
            _const_spec((1, d)),
        ],
        out_specs=pl.BlockSpec((tm, d), lambda i: (i, 0)),
        out_shape=jax.ShapeDtypeStruct((m, d), _F32),
        compiler_params=_params("parallel"),
        name="ple_step",
    )(x, p, norm_pre.reshape(1, d), w_gate, w_proj, norm_post.reshape(1, d))


def _conv_kernel(x_ref, npre_ref, w_ref, cw_ref, o_ref, carry_ref, *, seq_tiles, rb):
    first = (pl.program_id(0) % seq_tiles) == 0
    tm = x_ref.shape[0]
    cwid = o_ref.shape[1]

    @pl.when(first)
    def _():
        carry_ref[...] = jnp.zeros_like(carry_ref)

    cw = cw_ref[...]
    row = lax.broadcasted_iota(jnp.int32, (rb, cwid), 0)
    tail = carry_ref[...]
    for r in range(tm // rb):
        rows = slice(r * rb, (r + 1) * rb)
        h = _rmsnorm(x_ref[rows, :], npre_ref[...]).astype(_MXU_DTYPE)
        proj = _dot(h, w_ref[...])
        xin, b_gate, c_gate = proj[:, :cwid], proj[:, cwid:2 * cwid], proj[:, 2 * cwid:]
        u = c_gate * xin
        prev1 = tail[7:8, :]
        prev2 = tail[6:7, :]
        u1 = jnp.where(row == 0, prev1, pltpu.roll(u, 1, 0))
        u2 = jnp.where(row == 0, prev2, jnp.where(row == 1, prev1, pltpu.roll(u, 2, 0)))
        y = cw[2:3, :] * u + cw[1:2, :] * u1 + cw[0:1, :] * u2
        o_ref[rows, :] = (b_gate * y).astype(o_ref.dtype)
        tail = u[rb - 8:, :]
    carry_ref[...] = tail


def _conv_branch(x, norm_pre, w_in_conv, conv_w, seq):
    m, d = x.shape
    cwid = conv_w.shape[1]
    tm = _pick(seq, _CONV_TM)
    cw = jnp.zeros((8, cwid), _F32).at[:CONV_K].set(conv_w)
    return pl.pallas_call(
        functools.partial(_conv_kernel, seq_tiles=seq // tm, rb=min(tm, _ROW_BLOCK)),
        grid=(m // tm,),
        in_specs=[
            pl.BlockSpec((tm, d), lambda i: (i, 0)),
            _const_spec((1, d)),
            _const_spec((d, 3 * cwid)),
            _const_spec((8, cwid)),
        ],
        out_specs=pl.BlockSpec((tm, cwid), lambda i: (i, 0)),
        out_shape=jax.ShapeDtypeStruct((m, cwid), _MXU_DTYPE),
        scratch_shapes=[pltpu.VMEM((8, cwid), _F32)],
        compiler_params=_params("arbitrary"),
        name="conv_branch",
    )(x, norm_pre.reshape(1, d), w_in_conv, cw)


def _attn_kernel(sink_ref, x_ref, npre_ref, w_ref, o_ref, q_s, k_s, v_s, bias_s, *, seq_tiles):
    blk = ATTN_BLOCK
    tm = x_ref.shape[0]
    nq = ATTN_HEADS * HEAD_DIM
    nkv = ATTN_KV_HEADS * HEAD_DIM
    first = (pl.program_id(0) % seq_tiles) == 0
    group = ATTN_HEADS // ATTN_KV_HEADS

    @pl.when(pl.program_id(0) == 0)
    def _():
        qi = lax.broadcasted_iota(jnp.int32, (blk, 2 * blk), 0)
        ki = lax.broadcasted_iota(jnp.int32, (blk, 2 * blk), 1)
        dist = qi + blk - ki
        band = (dist >= 0) & (dist < blk)
        distf = dist.astype(_F32)
        for head in range(ATTN_HEADS):
            slope = 2.0 ** (-8.0 * (head + 1) / ATTN_HEADS)
            bias_s[0, head] = jnp.where(band, -slope * distf, NEG_BIG)
            bias_s[1, head] = jnp.where(band & (ki >= blk), -slope * distf, NEG_BIG)

    h = _rmsnorm(x_ref[...], npre_ref[...]).astype(_MXU_DTYPE)
    qkv = _dot(h, w_ref[...])
    q_s[...] = (qkv[:, :nq] * (HEAD_DIM ** -0.5)).astype(q_s.dtype)

    @pl.when(first)
    def _():
        k_s[0:blk, :] = jnp.zeros((blk, nkv), _F32)
        v_s[0:blk, :] = jnp.zeros((blk, nkv), _F32)

    k_s[blk:blk + tm, :] = qkv[:, nq:nq + nkv]
    v_s[blk:blk + tm, :] = qkv[:, nq + nkv:nq + 2 * nkv]

    lo = lax.broadcasted_iota(jnp.int32, (1, _LANES), 1) < _HALF

    def block(b, carry):
        r0 = pl.multiple_of(b * blk, blk)
        kc = k_s[pl.ds(r0, 2 * blk), :]
        vc = v_s[pl.ds(r0, 2 * blk), :]
        kr = pltpu.roll(kc, _HALF, 1)
        vr = pltpu.roll(vc, _HALF, 1)
        k_dup = [jnp.where(lo, kc, kr).astype(_MXU_DTYPE), jnp.where(lo, kr, kc).astype(_MXU_DTYPE)]
        v_half = [
            [jnp.where(lo, vc, 0.0).astype(_MXU_DTYPE), jnp.where(lo, 0.0, vr).astype(_MXU_DTYPE)],
            [jnp.where(lo, vr, 0.0).astype(_MXU_DTYPE), jnp.where(lo, 0.0, vc).astype(_MXU_DTYPE)],
        ]
        no_prev = jnp.logical_and(first, b == 0).astype(jnp.int32)
        def scores(head):
            pair, half = divmod(head, 2)
            qp = q_s[pl.ds(r0, blk), pair * _LANES:(pair + 1) * _LANES]
            zero = jnp.zeros_like(qp)
            qm = jnp.where(lo, qp, zero) if half == 0 else jnp.where(lo, zero, qp)
            s = _dot_nt(qm, k_dup[head // group]) + bias_s[no_prev, head]
            sink = sink_ref[head]
            mx = jnp.maximum(jnp.max(s, axis=-1, keepdims=True), sink)
            pr = jnp.exp(s - mx)
            denom = jnp.sum(pr, axis=-1, keepdims=True) + jnp.exp(sink - mx)
            return pr.astype(_MXU_DTYPE), denom

        acc = {}

        def values(head, pr, denom):
            pair, half = divmod(head, 2)
            o = _dot(pr, v_half[head // group][half]) / denom
            if half == 0:
                acc[pair] = o
            else:
                o_ref[pl.ds(r0, blk), pair * _LANES:(pair + 1) * _LANES] = (acc.pop(pair) + o).astype(o_ref.dtype)

        pending = []
        for head in range(ATTN_HEADS):
            pending.append((head,) + scores(head))
            if len(pending) > _ATTN_SKEW:
                values(*pending.pop(0))
        while pending:
            values(*pending.pop(0))
        return carry

    lax.fori_loop(0, tm // blk, block, 0, unroll=2)
    k_s[0:blk, :] = k_s[tm:tm + blk, :]
    v_s[0:blk, :] = v_s[tm:tm + blk, :]


def _attn_branch(x, norm_pre, w_in_attn, sinks, seq):
    m, d = x.shape
    nq = ATTN_HEADS * HEAD_DIM
    nkv = ATTN_KV_HEADS * HEAD_DIM
    tm = _pick(seq, _ATTN_TM)
    assert tm % ATTN_BLOCK == 0 and nkv == _LANES
    return pl.pallas_call(
        functools.partial(_attn_kernel, seq_tiles=seq // tm),
        grid=(m // tm,),
        in_specs=[
            pl.BlockSpec(memory_space=pltpu.SMEM),
            pl.BlockSpec((tm, d), lambda i: (i, 0)),
            _const_spec((1, d)),
            _const_spec((d, nq + 2 * nkv)),
        ],
        out_specs=pl.BlockSpec((tm, nq), lambda i: (i, 0)),
        out_shape=jax.ShapeDtypeStruct((m, nq), _MXU_DTYPE),
        scratch_shapes=[
            pltpu.VMEM((tm, nq), _MXU_DTYPE),
            pltpu.VMEM((tm + ATTN_BLOCK, nkv), _F32),
            pltpu.VMEM((tm + ATTN_BLOCK, nkv), _F32),
            pltpu.VMEM((2, ATTN_HEADS, ATTN_BLOCK, 2 * ATTN_BLOCK), _F32),
        ],
        compiler_params=_params("arbitrary"),
        name="attn_branch",
    )(sinks, x, norm_pre.reshape(1, d), w_in_attn)


def _hgrn_kernel(x_ref, npre_ref, w_ref, lbl_ref, nw_ref, o_ref, pr_s, st_s, bak_s, *, seq_tiles, layer):
    ch = _HGRN_CHUNK
    tm = x_ref.shape[0]
    nk = HGRN_HEADS * HGRN_DK
    nv = HGRN_HEADS * HGRN_DV
    first = (pl.program_id(0) % seq_tiles) == 0

    @pl.when(first)
    def _():
        st_s[...] = jnp.zeros_like(st_s)

    h = _rmsnorm(x_ref[...], npre_ref[...]).astype(_MXU_DTYPE)
    pr_s[...] = _dot(h, w_ref[...])

    lg = lbl_ref[...]
    e = jnp.exp(lg - jnp.max(lg, axis=0, keepdims=True))
    sm = e / jnp.sum(e, axis=0, keepdims=True)
    lb = jnp.zeros((1, nk), _F32)
    for prev in range(layer):
        lb = lb + sm[prev:prev + 1, :]
    lb_floor = jnp.maximum(lb, LB_FLOOR)
    one_minus_lb = 1.0 - lb

    lo = lax.broadcasted_iota(jnp.int32, (1, _LANES), 1) < _HALF
    rid = lax.broadcasted_iota(jnp.int32, (ch, nk), 0)
    tril = lax.broadcasted_iota(jnp.int32, (ch, ch), 0) >= lax.broadcasted_iota(jnp.int32, (ch, ch), 1)
    nw = nw_ref[...]

    def chunk(c, a_min, *, exact):
        rows = pl.ds(pl.multiple_of(c * ch, ch), ch)
        q = pr_s[rows, 0:nk]
        z = pr_s[rows, nk:2 * nk]
        qs = q * jax.nn.sigmoid(q)
        sig_pos = jax.nn.sigmoid(z)
        sig_neg = 1.0 - sig_pos
        log_f = jnp.log(sig_pos + lb_floor * sig_neg)
        kk = one_minus_lb * sig_neg
        a = log_f
        step = 1
        while step < ch:
            a = a + jnp.where(rid >= step, pltpu.roll(a, step, 0), 0.0)
            step *= 2
        a_last = a[ch - 1:ch, :]
        q_dec = (qs * jnp.exp(a)).astype(_MXU_DTYPE)
        k_end = (kk * jnp.exp(a_last - a)).astype(_MXU_DTYPE)
        d_end = jnp.exp(a_last)
        heads = range(HGRN_HEADS)
        sls = [slice(hd * HGRN_DK, (hd + 1) * HGRN_DK) for hd in heads]
        vps = [pr_s[rows, 2 * nk + pr * _LANES:2 * nk + (pr + 1) * _LANES] for pr in range(HGRN_HEADS // 2)]
        vpm = [v.astype(_MXU_DTYPE) for v in vps]
        if exact:
            sid = lax.broadcasted_iota(jnp.int32, (ch, nk), 0)
            sid_v = lax.broadcasted_iota(jnp.int32, (ch, _LANES), 0)

            def query_row(t, intra):
                a_t = jnp.sum(jnp.where(sid == t, a, 0.0), axis=0, keepdims=True)
                q_t = jnp.sum(jnp.where(sid == t, qs, 0.0), axis=0, keepdims=True)
                wts = q_t * jnp.exp(jnp.where(sid <= t, a_t - a, NEG_BIG)) * kk
                out = []
                for hd in heads:
                    col = jnp.sum(wts[:, sls[hd]], axis=-1, keepdims=True)
                    o_t = jnp.sum(col * vps[hd // 2], axis=0, keepdims=True)
                    out.append(jnp.where(sid_v == t, o_t, intra[hd]))
                return tuple(out)

            intra = lax.fori_loop(0, ch, query_row, tuple(jnp.zeros((ch, _LANES), _F32) for _ in heads))
        else:
            k_inv = (kk * jnp.exp(-a)).astype(_MXU_DTYPE)
            att = [_dot_nt(q_dec[:, sls[hd]], k_inv[:, sls[hd]]) for hd in heads]
        upd = [_dot_tn(vpm[hd // 2], k_end[:, sls[hd]]) for hd in heads]
        states = [st_s[hd] for hd in heads]
        inter = [_dot_nt(q_dec[:, sls[hd]], states[hd].astype(_MXU_DTYPE)) for hd in heads]
        if not exact:
            intra = [_dot(jnp.where(tril, att[hd], 0.0).astype(_MXU_DTYPE), vpm[hd // 2]) for hd in heads]
        for hd in heads:
            st_s[hd] = states[hd] * d_end[:, sls[hd]] + upd[hd]
        for pair in range(HGRN_HEADS // 2):
            gp = pr_s[rows, 2 * nk + nv + pair * _LANES:2 * nk + nv + (pair + 1) * _LANES]
            outs = [inter[2 * pair + half] + intra[2 * pair + half] for half in range(2)]
            o = jnp.where(lo, outs[0], outs[1])
            osq = o * o
            ms_lo = jnp.sum(jnp.where(lo, osq, 0.0), axis=-1, keepdims=True) * (1.0 / HGRN_DV)
            ms_hi = jnp.sum(jnp.where(lo, 0.0, osq), axis=-1, keepdims=True) * (1.0 / HGRN_DV)
            rinv = jnp.where(lo, lax.rsqrt(ms_lo + NORM_EPS), lax.rsqrt(ms_hi + NORM_EPS))
            on = o * rinv * nw[:, pair * _LANES:(pair + 1) * _LANES]
            o_ref[rows, pair * _LANES:(pair + 1) * _LANES] = (on * (gp * jax.nn.sigmoid(gp))).astype(o_ref.dtype)
        return jnp.minimum(a_min, a_last)

    bak_s[...] = st_s[...]
    a_min = lax.fori_loop(0, tm // ch, functools.partial(chunk, exact=False), jnp.zeros((1, nk), _F32),
                          unroll=_HGRN_UNROLL)

    @pl.when(jnp.min(a_min) < -_HGRN_SAFE_LOG_DECAY)
    def _():
        st_s[...] = bak_s[...]
        lax.fori_loop(0, tm // ch, functools.partial(chunk, exact=True), jnp.zeros((1, nk), _F32))


def _hgrn_branch(x, norm_pre, w_in_hgrn, lb_logits, norm_w, seq, layer):
    m, d = x.shape
    nk = HGRN_HEADS * HGRN_DK
    nv = HGRN_HEADS * HGRN_DV
    depth = lb_logits.shape[0]
    tm = _pick(seq, _HGRN_TM)
    assert HGRN_DK == _LANES and 2 * HGRN_DV == _LANES and tm % _HGRN_CHUNK == 0
    nw = jnp.tile(norm_w.reshape(1, HGRN_DV), (1, HGRN_HEADS))
    return pl.pallas_call(
        functools.partial(_hgrn_kernel, seq_tiles=seq // tm, layer=layer),
        grid=(m // tm,),
        in_specs=[
            pl.BlockSpec((tm, d), lambda i: (i, 0)),
            _const_spec((1, d)),
            _const_spec((d, 2 * nk + 2 * nv)),
            _const_spec((depth, nk)),
            _const_spec((1, nv)),
        ],
        out_specs=pl.BlockSpec((tm, nv), lambda i: (i, 0)),
        out_shape=jax.ShapeDtypeStruct((m, nv), _MXU_DTYPE),
        scratch_shapes=[
            pltpu.VMEM((tm, 2 * nk + 2 * nv), _F32),
            pltpu.VMEM((HGRN_HEADS, _LANES, HGRN_DK), _F32),
            pltpu.VMEM((HGRN_HEADS, _LANES, HGRN_DK), _F32),
        ],
        compiler_params=_params("arbitrary"),
        name="hgrn_branch",
    )(x, norm_pre.reshape(1, d), w_in_hgrn, lb_logits, nw)


def _merge_kernel(x_ref, npre_ref, wgc_ref, wga_ref, wgh_ref, yc_ref, ya_ref, yh_ref,
                  wbc_ref, wba_ref, wbh_ref, wo_ref, npost_ref, o_ref, h_ref, *, rb):
    j = pl.program_id(1)
    last = pl.num_programs(1) - 1

    @pl.when(j == 0)
    def _():
        _prenorm_rows(x_ref, npre_ref, h_ref, o_ref)

    for r in range(x_ref.shape[0] // rb):
        rows = slice(r * rb, (r + 1) * rb)
        h = h_ref[rows, :]
        merged = (jax.nn.sigmoid(_dot(h, wgc_ref[...])) * _dot(yc_ref[rows, :], wbc_ref[...])
                  + jax.nn.sigmoid(_dot(h, wga_ref[...])) * _dot(ya_ref[rows, :], wba_ref[...])
                  + jax.nn.sigmoid(_dot(h, wgh_ref[...])) * _dot(yh_ref[rows, :], wbh_ref[...]))
        o_ref[rows, :] += _dot(merged.astype(_MXU_DTYPE), wo_ref[...])

    @pl.when(j == last)
    def _():
        _residual_rows(x_ref, npost_ref, o_ref, 1.0)


def _merge_step(x, norm_pre, w_gates, y_conv, y_attn, y_hgrn, wb_conv, wb_attn, wb_hgrn, w_o, norm_post):
    m, d = x.shape
    tm, tn = _pick(m, _MIX_TM), _pick(d, _MIX_TN)
    nt = d // tn
    nc, na, nh = y_conv.shape[1], y_attn.shape[1], y_hgrn.shape[1]
    return pl.pallas_call(
        functools.partial(_merge_kernel, rb=min(tm, _FFN_RB)),
        grid=(m // tm, nt),
        in_specs=[
            pl.BlockSpec((tm, d), lambda i, j: (i, 0)),
            pl.BlockSpec((1, d), lambda i, j: (0, 0)),
            pl.BlockSpec((d, tn), lambda i, j: (0, j)),
            pl.BlockSpec((d, tn), lambda i, j: (0, j + nt)),
            pl.BlockSpec((d, tn), lambda i, j: (0, j + 2 * nt)),
            pl.BlockSpec((tm, nc), lambda i, j: (i, 0)),
            pl.BlockSpec((tm, na), lambda i, j: (i, 0)),
            pl.BlockSpec((tm, nh), lambda i, j: (i, 0)),
            pl.BlockSpec((nc, tn), lambda i, j: (0, j)),
            pl.BlockSpec((na, tn), lambda i, j: (0, j)),
            pl.BlockSpec((nh, tn), lambda i, j: (0, j)),
            pl.BlockSpec((tn, d), lambda i, j: (j, 0)),
            pl.BlockSpec((1, d), lambda i, j: (0, 0)),
        ],
        out_specs=pl.BlockSpec((tm, d), lambda i, j: (i, 0)),
        out_shape=jax.ShapeDtypeStruct((m, d), _F32),
        scratch_shapes=[pltpu.VMEM((tm, d), _MXU_DTYPE)],
        compiler_params=_params("parallel", "arbitrary"),
        name="merge_step",
    )(x, norm_pre.reshape(1, d), w_gates, w_gates, w_gates, y_conv, y_attn, y_hgrn,
      wb_conv, wb_attn, wb_hgrn, w_o, norm_post.reshape(1, d))


def kernel(x, p, ffn1_norm_pre, ffn1_w_gu, ffn1_w_down, ffn1_norm_post, mix_norm_pre, w_in, conv_w, attn_sinks, hgrn_lb_logits, hgrn_norm, w_branch_conv, w_branch_attn, w_branch_hgrn, w_o, mix_norm_post, ffn2_norm_pre, ffn2_w_gu, ffn2_w_down, ffn2_norm_post, ple_norm_pre, w_ple_gate, w_ple_proj, ple_norm_post):
    batch, seq, d = x.shape
    depth = p.shape[0]
    m = batch * seq
    n_conv = 3 * CONV_WIDTH
    n_attn = (ATTN_HEADS + 2 * ATTN_KV_HEADS) * HEAD_DIM
    n_hgrn = 2 * HGRN_HEADS * (HGRN_DK + HGRN_DV)
    o_attn = n_conv
    o_hgrn = o_attn + n_attn
    o_gate = o_hgrn + n_hgrn

    xf = x.reshape(m, d)
    for l in range(depth):
        xf = _ffn_half_step(xf, ffn1_norm_pre[l], _to_mxu(ffn1_w_gu, l), _to_mxu(ffn1_w_down, l), ffn1_norm_post[l])
        w_conv, w_attn, w_hgrn, w_gates = _split_to_mxu(w_in, l, (n_conv, n_attn, n_hgrn, 3 * d))
        y_conv = _conv_branch(xf, mix_norm_pre[l], w_conv, conv_w[l], seq)
        y_attn = _attn_branch(xf, mix_norm_pre[l], w_attn, attn_sinks[l], seq)
        y_hgrn = _hgrn_branch(xf, mix_norm_pre[l], w_hgrn, hgrn_lb_logits, hgrn_norm[l], seq, l)
        xf = _merge_step(xf, mix_norm_pre[l], w_gates, y_conv, y_attn, y_hgrn,
                         _to_mxu(w_branch_conv, l), _to_mxu(w_branch_attn, l), _to_mxu(w_branch_hgrn, l),
                         _to_mxu(w_o, l), mix_norm_post[l])
        xf = _ffn_half_step(xf, ffn2_norm_pre[l], _to_mxu(ffn2_w_gu, l), _to_mxu(ffn2_w_down, l), ffn2_norm_post[l])
        xf = _ple_step(xf, p.reshape(depth, m, -1), l, ple_norm_pre[l], _to_mxu(w_ple_gate, l),
                       _to_mxu(w_ple_proj, l), ple_norm_post[l])
    return xf.reshape(batch, seq, d)
```

```python
import functools
import math

import jax
import jax.numpy as jnp
from jax import lax
from jax.experimental import pallas as pl
from jax.experimental.pallas import tpu as pltpu

_MXU_DTYPE = jnp.bfloat16
_F32 = jnp.float32

NORM_EPS = 1e-6
NEG_BIG = -1e30
LB_FLOOR = 1e-20

CONV_WIDTH = 512
CONV_K = 3
ATTN_HEADS = 16
ATTN_KV_HEADS = 2
HEAD_DIM = 64
ATTN_BLOCK = 128
HGRN_HEADS = 8
HGRN_DK = 128
HGRN_DV = 64

_LANES = 128
_HALF = 64

_HGRN_CHUNK = 64
_HGRN_SAFE_LOG_DECAY = 80.0

_FFN_TM = 1024
_FFN_RB = 512
_FFN_TF = 512
_MIX_TM = 512
_MIX_TN = 512
_PLE_TM = 1024
_CONV_TM = 512
_ROW_BLOCK = 256
_CAST_ROWS = 512
_CAST_COLS = 2816
_SPLIT_ROWS = 128
_ATTN_TM = 512
_HGRN_TM = 512
_HGRN_UNROLL = 2
_ATTN_SKEW = 6

_VMEM_LIMIT = 60 * 1024 * 1024


def _pick(n, pref):
    t = min(n, pref)
    assert n % t == 0, (n, pref)
    return t


def _rmsnorm(xf, w):
    return xf * lax.rsqrt(jnp.mean(xf * xf, axis=-1, keepdims=True) + NORM_EPS) * w


def _reduce_steps(x_ref, npre_ref, npost_ref, o_ref, h_ref, partial, *, n_steps, rb, rb_edge, scale):
    j = pl.program_id(1)
    tm = x_ref.shape[0]

    def run(first, last, rows_per_block):
        w_post = npost_ref[...] * scale
        for r in range(tm // rows_per_block):
            rows = slice(r * rows_per_block, (r + 1) * rows_per_block)
            if first:
                h = _rmsnorm(x_ref[rows, :], npre_ref[...]).astype(h_ref.dtype)
                h_ref[rows, :] = h
            else:
                h = h_ref[rows, :]
            acc = partial(h, rows)
            if not first:
                acc = o_ref[rows, :] + acc
            o_ref[rows, :] = x_ref[rows, :] + _rmsnorm(acc, w_post) if last else acc

    if n_steps == 1:
        run(True, True, rb_edge)
        return
    pl.when(j == 0)(functools.partial(run, True, False, rb_edge))
    if n_steps > 2:
        pl.when(jnp.logical_and(j > 0, j < n_steps - 1))(functools.partial(run, False, False, rb))
    pl.when(j == n_steps - 1)(functools.partial(run, False, True, rb_edge))


def _dot(a, b):
    return jnp.dot(a, b, preferred_element_type=_F32)


def _dot_nt(a, b):
    return lax.dot_general(a, b, (((1,), (1,)), ((), ())), preferred_element_type=_F32)


def _dot_tn(a, b):
    return lax.dot_general(a, b, (((0,), (0,)), ((), ())), preferred_element_type=_F32)


def _params(*sem):
    return pltpu.CompilerParams(dimension_semantics=sem, vmem_limit_bytes=_VMEM_LIMIT)


def _const_spec(shape):
    nd = len(shape)
    return pl.BlockSpec(shape, lambda *_: (0,) * nd, pipeline_mode=pl.Buffered(1))


def _cast_kernel(w_ref, o_ref):
    o_ref[...] = w_ref[...].astype(o_ref.dtype)


def _to_mxu(w, layer, col0=0, width=None):
    _, r, c = w.shape
    width = c if width is None else width
    span = math.gcd(col0, width)
    tc = max(t for t in range(_LANES, min(span, _CAST_COLS) + 1, _LANES) if span % t == 0)
    tr = _pick(r, _CAST_ROWS)
    return pl.pallas_call(
        _cast_kernel,
        grid=(r // tr, width // tc),
        in_specs=[pl.BlockSpec((None, tr, tc), lambda i, j: (layer, i, j + col0 // tc))],
        out_specs=pl.BlockSpec((tr, tc), lambda i, j: (i, j)),
        out_shape=jax.ShapeDtypeStruct((r, width), _MXU_DTYPE),
        compiler_params=_params("parallel", "parallel"),
        name="weight_cast",
    )(w)


def _split_cast_kernel(w_ref, *o_refs):
    col = 0
    for o_ref in o_refs:
        o_ref[...] = w_ref[:, col:col + o_ref.shape[1]].astype(o_ref.dtype)
        col += o_ref.shape[1]


def _split_to_mxu(w, layer, widths):
    _, r, c = w.shape
    assert sum(widths) == c and all(n % _LANES == 0 for n in widths)
    tr = _pick(r, _SPLIT_ROWS)
    return pl.pallas_call(
        _split_cast_kernel,
        grid=(r // tr,),
        in_specs=[pl.BlockSpec((None, tr, c), lambda i: (layer, i, 0))],
        out_specs=[pl.BlockSpec((tr, n), lambda i: (i, 0)) for n in widths],
        out_shape=[jax.ShapeDtypeStruct((r, n), _MXU_DTYPE) for n in widths],
        compiler_params=_params("parallel"),
        name="weight_split_cast",
    )(w)


def _ffn_kernel(x_ref, npre_ref, wg_ref, wu_ref, wd_ref, npost_ref, o_ref, h_ref, **steps):
    def partial(h, rows):
        g = _dot(h, wg_ref[...])
        u = _dot(h, wu_ref[...])
        a = (g * jax.nn.sigmoid(g) * u).astype(_MXU_DTYPE)
        return _dot(a, wd_ref[...])

    _reduce_steps(x_ref, npre_ref, npost_ref, o_ref, h_ref, partial, scale=0.5, **steps)


def _ffn_half_step(x, norm_pre, w_gu, w_down, norm_post):
    m, d = x.shape
    f = w_down.shape[0]
    tm, tf = _pick(m, _FFN_TM), _pick(f, _FFN_TF)
    nf = f // tf
    return pl.pallas_call(
        functools.partial(_ffn_kernel, n_steps=nf, rb=min(tm, _FFN_RB), rb_edge=min(tm, _FFN_RB)),
        grid=(m // tm, nf),
        in_specs=[
            pl.BlockSpec((tm, d), lambda i, j: (i, 0)),
            pl.BlockSpec((1, d), lambda i, j: (0, 0)),
            pl.BlockSpec((d, tf), lambda i, j: (0, j)),
            pl.BlockSpec((d, tf), lambda i, j: (0, j + nf)),
            pl.BlockSpec((tf, d), lambda i, j: (j, 0)),
            pl.BlockSpec((1, d), lambda i, j: (0, 0)),
        ],
        out_specs=pl.BlockSpec((tm, d), lambda i, j: (i, 0)),
        out_shape=jax.ShapeDtypeStruct((m, d), _F32),
        scratch_shapes=[pltpu.VMEM((tm, d), _MXU_DTYPE)],
        compiler_params=_params("parallel", "arbitrary"),
        name="ffn_half_step",
    )(x, norm_pre.reshape(1, d), w_gu, w_gu, w_down, norm_post.reshape(1, d))


def _ple_kernel(x_ref, p_ref, npre_ref, wg_ref, wp_ref, npost_ref, o_ref, *, rb):
    for r in range(x_ref.shape[0] // rb):
        rows = slice(r * rb, (r + 1) * rb)
        x = x_ref[rows, :]
        hp = _rmsnorm(x, npre_ref[...]).astype(_MXU_DTYPE)
        gate = jax.nn.sigmoid(_dot(hp, wg_ref[...]))
        proj = _dot(p_ref[rows, :].astype(_MXU_DTYPE), wp_ref[...])
        o_ref[rows, :] = x + _rmsnorm(gate * proj, npost_ref[...])


def _ple_step(x, p, layer, norm_pre, w_gate, w_proj, norm_post):
    m, d = x.shape
    dp = p.shape[2]
    tm = _pick(m, _PLE_TM)
    return pl.pallas_call(
        functools.partial(_ple_kernel, rb=min(tm, _FFN_RB)),
        grid=(m // tm,),
        in_specs=[
            pl.BlockSpec((tm, d), lambda i: (i, 0)),
            pl.BlockSpec((None, tm, dp), lambda i: (layer, i, 0)),
            _const_spec((1, d)),
            _const_spec((d, d)),
            _const_spec((dp, d)),
            _const_spec((1, d)),
        ],
        out_specs=pl.BlockSpec((tm, d), lambda i: (i, 0)),
        out_shape=jax.ShapeDtypeStruct((m, d), _F32),
        compiler_params=_params("parallel"),
        name="ple_step",
    )(x, p, norm_pre.reshape(1, d), w_gate, w_proj, norm_post.reshape(1, d))


def _conv_kernel(x_ref, npre_ref, w_ref, cw_ref, o_ref, carry_ref, *, seq_tiles, rb):
    first = (pl.program_id(0) % seq_tiles) == 0
    tm = x_ref.shape[0]
    cwid = o_ref.shape[1]

    @pl.when(first)
    def _():
        carry_ref[...] = jnp.zeros_like(carry_ref)

    cw = cw_ref[...]
    row = lax.broadcasted_iota(jnp.int32, (rb, cwid), 0)
    tail = carry_ref[...]
    for r in range(tm // rb):
        rows = slice(r * rb, (r + 1) * rb)
        h = _rmsnorm(x_ref[rows, :], npre_ref[...]).astype(_MXU_DTYPE)
        proj = _dot(h, w_ref[...])
        xin, b_gate, c_gate = proj[:, :cwid], proj[:, cwid:2 * cwid], proj[:, 2 * cwid:]
        u = c_gate * xin
        prev1 = tail[7:8, :]
        prev2 = tail[6:7, :]
        u1 = jnp.where(row == 0, prev1, pltpu.roll(u, 1, 0))
        u2 = jnp.where(row == 0, prev2, jnp.where(row == 1, prev1, pltpu.roll(u, 2, 0)))
        y = cw[2:3, :] * u + cw[1:2, :] * u1 + cw[0:1, :] * u2
        o_ref[rows, :] = (b_gate * y).astype(o_ref.dtype)
        tail = u[rb - 8:, :]
    carry_ref[...] = tail


def _conv_branch(x, norm_pre, w_in_conv, conv_w, seq):
    m, d = x.shape
    cwid = conv_w.shape[1]
    tm = _pick(seq, _CONV_TM)
    cw = jnp.zeros((8, cwid), _F32).at[:CONV_K].set(conv_w)
    return pl.pallas_call(
        functools.partial(_conv_kernel, seq_tiles=seq // tm, rb=min(tm, _ROW_BLOCK)),
        grid=(m // tm,),
        in_specs=[
            pl.BlockSpec((tm, d), lambda i: (i, 0)),
            _const_spec((1, d)),
            _const_spec((d, 3 * cwid)),
            _const_spec((8, cwid)),
        ],
        out_specs=pl.BlockSpec((tm, cwid), lambda i: (i, 0)),
        out_shape=jax.ShapeDtypeStruct((m, cwid), _MXU_DTYPE),
        scratch_shapes=[pltpu.VMEM((8, cwid), _F32)],
        compiler_params=_params("arbitrary"),
        name="conv_branch",
    )(x, norm_pre.reshape(1, d), w_in_conv, cw)


def _attn_kernel(sink_ref, x_ref, npre_ref, w_ref, o_ref, q_s, k_s, v_s, bias_s, *, seq_tiles):
    blk = ATTN_BLOCK
    tm = x_ref.shape[0]
    nq = ATTN_HEADS * HEAD_DIM
    nkv = ATTN_KV_HEADS * HEAD_DIM
    first = (pl.program_id(0) % seq_tiles) == 0
    group = ATTN_HEADS // ATTN_KV_HEADS

    @pl.when(pl.program_id(0) == 0)
    def _():
        qi = lax.broadcasted_iota(jnp.int32, (blk, 2 * blk), 0)
        ki = lax.broadcasted_iota(jnp.int32, (blk, 2 * blk), 1)
        dist = qi + blk - ki
        band = (dist >= 0) & (dist < blk)
        distf = dist.astype(_F32)
        for head in range(ATTN_HEADS):
            slope = 2.0 ** (-8.0 * (head + 1) / ATTN_HEADS)
            bias_s[0, head] = jnp.where(band, -slope * distf, NEG_BIG)
            bias_s[1, head] = jnp.where(band & (ki >= blk), -slope * distf, NEG_BIG)

    h = _rmsnorm(x_ref[...], npre_ref[...]).astype(_MXU_DTYPE)
    qkv = _dot(h, w_ref[...])
    q_s[...] = (qkv[:, :nq] * (HEAD_DIM ** -0.5)).astype(q_s.dtype)

    @pl.when(first)
    def _():
        k_s[0:blk, :] = jnp.zeros((blk, nkv), _F32)
        v_s[0:blk, :] = jnp.zeros((blk, nkv), _F32)

    k_s[blk:blk + tm, :] = qkv[:, nq:nq + nkv]
    v_s[blk:blk + tm, :] = qkv[:, nq + nkv:nq + 2 * nkv]

    lo = lax.broadcasted_iota(jnp.int32, (1, _LANES), 1) < _HALF

    def block(b, carry):
        r0 = pl.multiple_of(b * blk, blk)
        kc = k_s[pl.ds(r0, 2 * blk), :]
        vc = v_s[pl.ds(r0, 2 * blk), :]
        kr = pltpu.roll(kc, _HALF, 1)
        vr = pltpu.roll(vc, _HALF, 1)
        k_dup = [jnp.where(lo, kc, kr).astype(_MXU_DTYPE), jnp.where(lo, kr, kc).astype(_MXU_DTYPE)]
        v_half = [
            [jnp.where(lo, vc, 0.0).astype(_MXU_DTYPE), jnp.where(lo, 0.0, vr).astype(_MXU_DTYPE)],
            [jnp.where(lo, vr, 0.0).astype(_MXU_DTYPE), jnp.where(lo, 0.0, vc).astype(_MXU_DTYPE)],
        ]
        no_prev = jnp.logical_and(first, b == 0).astype(jnp.int32)
        def scores(head):
            pair, half = divmod(head, 2)
            qp = q_s[pl.ds(r0, blk), pair * _LANES:(pair + 1) * _LANES]
            zero = jnp.zeros_like(qp)
            qm = jnp.where(lo, qp, zero) if half == 0 else jnp.where(lo, zero, qp)
            s = _dot_nt(qm, k_dup[head // group]) + bias_s[no_prev, head]
            sink = sink_ref[head]
            mx = jnp.maximum(jnp.max(s, axis=-1, keepdims=True), sink)
            pr = jnp.exp(s - mx)
            denom = jnp.sum(pr, axis=-1, keepdims=True) + jnp.exp(sink - mx)
            return pr.astype(_MXU_DTYPE), denom

        acc = {}

        def values(head, pr, denom):
            pair, half = divmod(head, 2)
            o = _dot(pr, v_half[head // group][half]) / denom
            if half == 0:
                acc[pair] = o
            else:
                o_ref[pl.ds(r0, blk), pair * _LANES:(pair + 1) * _LANES] = (acc.pop(pair) + o).astype(o_ref.dtype)

        pending = []
        for head in range(ATTN_HEADS):
            pending.append((head,) + scores(head))
            if len(pending) > _ATTN_SKEW:
                values(*pending.pop(0))
        while pending:
            values(*pending.pop(0))
        return carry

    lax.fori_loop(0, tm // blk, block, 0, unroll=2)
    k_s[0:blk, :] = k_s[tm:tm + blk, :]
    v_s[0:blk, :] = v_s[tm:tm + blk, :]


def _attn_branch(x, norm_pre, w_in_attn, sinks, seq):
    m, d = x.shape
    nq = ATTN_HEADS * HEAD_DIM
    nkv = ATTN_KV_HEADS * HEAD_DIM
    tm = _pick(seq, _ATTN_TM)
    assert tm % ATTN_BLOCK == 0 and nkv == _LANES
    return pl.pallas_call(
        functools.partial(_attn_kernel, seq_tiles=seq // tm),
        grid=(m // tm,),
        in_specs=[
            pl.BlockSpec(memory_space=pltpu.SMEM),
            pl.BlockSpec((tm, d), lambda i: (i, 0)),
            _const_spec((1, d)),
            _const_spec((d, nq + 2 * nkv)),
        ],
        out_specs=pl.BlockSpec((tm, nq), lambda i: (i, 0)),
        out_shape=jax.ShapeDtypeStruct((m, nq), _MXU_DTYPE),
        scratch_shapes=[
            pltpu.VMEM((tm, nq), _MXU_DTYPE),
            pltpu.VMEM((tm + ATTN_BLOCK, nkv), _F32),
            pltpu.VMEM((tm + ATTN_BLOCK, nkv), _F32),
            pltpu.VMEM((2, ATTN_HEADS, ATTN_BLOCK, 2 * ATTN_BLOCK), _F32),
        ],
        compiler_params=_params("arbitrary"),
        name="attn_branch",
    )(sinks, x, norm_pre.reshape(1, d), w_in_attn)


def _hgrn_kernel(x_ref, npre_ref, w_ref, lbl_ref, nw_ref, o_ref, pr_s, st_s, bak_s, *, seq_tiles, layer):
    ch = _HGRN_CHUNK
    tm = x_ref.shape[0]
    nk = HGRN_HEADS * HGRN_DK
    nv = HGRN_HEADS * HGRN_DV
    first = (pl.program_id(0) % seq_tiles) == 0

    @pl.when(first)
    def _():
        st_s[...] = jnp.zeros_like(st_s)

    lg = lbl_ref[...]
    e = jnp.exp(lg - jnp.max(lg, axis=0, keepdims=True))
    sm = e / jnp.sum(e, axis=0, keepdims=True)
    lb = jnp.zeros((1, nk), _F32)
    for prev in range(layer):
        lb = lb + sm[prev:prev + 1, :]
    lb_floor = jnp.maximum(lb, LB_FLOOR)
    one_minus_lb = 1.0 - lb

    lo = lax.broadcasted_iota(jnp.int32, (1, _LANES), 1) < _HALF
    rid = lax.broadcasted_iota(jnp.int32, (ch, nk), 0)
    tril = lax.broadcasted_iota(jnp.int32, (ch, ch), 0) >= lax.broadcasted_iota(jnp.int32, (ch, ch), 1)
    nw = nw_ref[...]

    def chunk(c, a_min, *, exact, fill=(None, None)):
        rows = slice(c * ch, (c + 1) * ch) if isinstance(c, int) else pl.ds(pl.multiple_of(c * ch, ch), ch)
        if fill[0] is not None:
            fill[0]()
        q = pr_s[rows, 0:nk]
        z = pr_s[rows, nk:2 * nk]
        qs = q * jax.nn.sigmoid(q)
        sig_pos = jax.nn.sigmoid(z)
        sig_neg = 1.0 - sig_pos
        log_f = jnp.log(sig_pos + lb_floor * sig_neg)
        kk = one_minus_lb * sig_neg
        a = log_f
        step = 1
        while step < ch:
            a = a + jnp.where(rid >= step, pltpu.roll(a, step, 0), 0.0)
            step *= 2
        a_last = a[ch - 1:ch, :]
        q_dec = (qs * jnp.exp(a)).astype(_MXU_DTYPE)
        k_end = (kk * jnp.exp(a_last - a)).astype(_MXU_DTYPE)
        d_end = jnp.exp(a_last)
        heads = range(HGRN_HEADS)
        sls = [slice(hd * HGRN_DK, (hd + 1) * HGRN_DK) for hd in heads]
        vps = [pr_s[rows, 2 * nk + pr * _LANES:2 * nk + (pr + 1) * _LANES] for pr in range(HGRN_HEADS // 2)]
        vpm = [v.astype(_MXU_DTYPE) for v in vps]
        if exact:
            sid = lax.broadcasted_iota(jnp.int32, (ch, nk), 0)
            sid_v = lax.broadcasted_iota(jnp.int32, (ch, _LANES), 0)

            def query_row(t, intra):
                a_t = jnp.sum(jnp.where(sid == t, a, 0.0), axis=0, keepdims=True)
                q_t = jnp.sum(jnp.where(sid == t, qs, 0.0), axis=0, keepdims=True)
                wts = q_t * jnp.exp(jnp.where(sid <= t, a_t - a, NEG_BIG)) * kk
                out = []
                for hd in heads:
                    col = jnp.sum(wts[:, sls[hd]], axis=-1, keepdims=True)
                    o_t = jnp.sum(col * vps[hd // 2], axis=0, keepdims=True)
                    out.append(jnp.where(sid_v == t, o_t, intra[hd]))
                return tuple(out)

            intra = lax.fori_loop(0, ch, query_row, tuple(jnp.zeros((ch, _LANES), _F32) for _ in heads))
        else:
            k_inv = (kk * jnp.exp(-a)).astype(_MXU_DTYPE)
            att = [_dot_nt(q_dec[:, sls[hd]], k_inv[:, sls[hd]]) for hd in heads]
        upd = [_dot_tn(vpm[hd // 2], k_end[:, sls[hd]]) for hd in heads]
        states = [st_s[hd] for hd in heads]
        inter = [_dot_nt(q_dec[:, sls[hd]], states[hd].astype(_MXU_DTYPE)) for hd in heads]
        if fill[1] is not None:
            fill[1]()
        if not exact:
            intra = [_dot(jnp.where(tril, att[hd], 0.0).astype(_MXU_DTYPE), vpm[hd // 2]) for hd in heads]
        for hd in heads:
            st_s[hd] = states[hd] * d_end[:, sls[hd]] + upd[hd]
        for pair in range(HGRN_HEADS // 2):
            gp = pr_s[rows, 2 * nk + nv + pair * _LANES:2 * nk + nv + (pair + 1) * _LANES]
            outs = [inter[2 * pair + half] + intra[2 * pair + half] for half in range(2)]
            o = jnp.where(lo, outs[0], outs[1])
            osq = o * o
            ms_lo = jnp.sum(jnp.where(lo, osq, 0.0), axis=-1, keepdims=True) * (1.0 / HGRN_DV)
            ms_hi = jnp.sum(jnp.where(lo, 0.0, osq), axis=-1, keepdims=True) * (1.0 / HGRN_DV)
            rinv = jnp.where(lo, lax.rsqrt(ms_lo + NORM_EPS), lax.rsqrt(ms_hi + NORM_EPS))
            on = o * rinv * nw[:, pair * _LANES:(pair + 1) * _LANES]
            o_ref[rows, pair * _LANES:(pair + 1) * _LANES] = (on * (gp * jax.nn.sigmoid(gp))).astype(o_ref.dtype)
        return jnp.minimum(a_min, a_last)

    bak_s[...] = st_s[...]

    rg = 2 * ch
    n_rg = tm // rg
    n_sl = 4
    scols = (2 * nk + 2 * nv) // n_sl

    def group_norm(g):
        return _rmsnorm(x_ref[g * rg:(g + 1) * rg, :], npre_ref[...]).astype(_MXU_DTYPE)

    def project(hg, g, s):
        cols = slice(s * scols, (s + 1) * scols)
        pr_s[g * rg:(g + 1) * rg, cols] = _dot(hg, w_ref[:, cols])

    hg = group_norm(0)
    for s in range(n_sl):
        project(hg, 0, s)
    a_min = jnp.zeros((1, nk), _F32)
    for c in range(tm // ch):
        g, part = divmod(c, 2)
        fill = (None, None)
        if g + 1 < n_rg:
            if part == 0:
                hg = group_norm(g + 1)
            fill = tuple(functools.partial(project, hg, g + 1, 2 * part + k) for k in range(2))
        a_min = chunk(c, a_min, exact=False, fill=fill)

    @pl.when(jnp.min(a_min) < -_HGRN_SAFE_LOG_DECAY)
    def _():
        st_s[...] = bak_s[...]
        lax.fori_loop(0, tm // ch, functools.partial(chunk, exact=True), jnp.zeros((1, nk), _F32))


def _hgrn_branch(x, norm_pre, w_in_hgrn, lb_logits, norm_w, seq, layer):
    m, d = x.shape
    nk = HGRN_HEADS * HGRN_DK
    nv = HGRN_HEADS * HGRN_DV
    depth = lb_logits.shape[0]
    tm = _pick(seq, _HGRN_TM)
    assert HGRN_DK == _LANES and 2 * HGRN_DV == _LANES and tm % _HGRN_CHUNK == 0
    nw = jnp.tile(norm_w.reshape(1, HGRN_DV), (1, HGRN_HEADS))
    return pl.pallas_call(
        functools.partial(_hgrn_kernel, seq_tiles=seq // tm, layer=layer),
        grid=(m // tm,),
        in_specs=[
            pl.BlockSpec((tm, d), lambda i: (i, 0)),
            _const_spec((1, d)),
            _const_spec((d, 2 * nk + 2 * nv)),
            _const_spec((depth, nk)),
            _const_spec((1, nv)),
        ],
        out_specs=pl.BlockSpec((tm, nv), lambda i: (i, 0)),
        out_shape=jax.ShapeDtypeStruct((m, nv), _MXU_DTYPE),
        scratch_shapes=[
            pltpu.VMEM((tm, 2 * nk + 2 * nv), _F32),
            pltpu.VMEM((HGRN_HEADS, _LANES, HGRN_DK), _F32),
            pltpu.VMEM((HGRN_HEADS, _LANES, HGRN_DK), _F32),
        ],
        compiler_params=_params("arbitrary"),
        name="hgrn_branch",
    )(x, norm_pre.reshape(1, d), w_in_hgrn, lb_logits, nw)


def _merge_kernel(x_ref, npre_ref, wgc_ref, wga_ref, wgh_ref, yc_ref, ya_ref, yh_ref,
                  wbc_ref, wba_ref, wbh_ref, wo_ref, npost_ref, o_ref, h_ref, **steps):
    def partial(h, rows):
        merged = (jax.nn.sigmoid(_dot(h, wgc_ref[...])) * _dot(yc_ref[rows, :], wbc_ref[...])
                  + jax.nn.sigmoid(_dot(h, wga_ref[...])) * _dot(ya_ref[rows, :], wba_ref[...])
                  + jax.nn.sigmoid(_dot(h, wgh_ref[...])) * _dot(yh_ref[rows, :], wbh_ref[...]))
        return _dot(merged.astype(_MXU_DTYPE), wo_ref[...])

    _reduce_steps(x_ref, npre_ref, npost_ref, o_ref, h_ref, partial, scale=1.0, **steps)


def _merge_step(x, norm_pre, w_gates, y_conv, y_attn, y_hgrn, wb_conv, wb_attn, wb_hgrn, w_o, norm_post):
    m, d = x.shape
    tm, tn = _pick(m, _MIX_TM), _pick(d, _MIX_TN)
    nt = d // tn
    nc, na, nh = y_conv.shape[1], y_attn.shape[1], y_hgrn.shape[1]
    return pl.pallas_call(
        functools.partial(_merge_kernel, n_steps=nt, rb=min(tm, _FFN_RB), rb_edge=min(tm, _ROW_BLOCK)),
        grid=(m // tm, nt),
        in_specs=[
            pl.BlockSpec((tm, d), lambda i, j: (i, 0)),
            pl.BlockSpec((1, d), lambda i, j: (0, 0)),
            pl.BlockSpec((d, tn), lambda i, j: (0, j)),
            pl.BlockSpec((d, tn), lambda i, j: (0, j + nt)),
            pl.BlockSpec((d, tn), lambda i, j: (0, j + 2 * nt)),
            pl.BlockSpec((tm, nc), lambda i, j: (i, 0)),
            pl.BlockSpec((tm, na), lambda i, j: (i, 0)),
            pl.BlockSpec((tm, nh), lambda i, j: (i, 0)),
            pl.BlockSpec((nc, tn), lambda i, j: (0, j)),
            pl.BlockSpec((na, tn), lambda i, j: (0, j)),
            pl.BlockSpec((nh, tn), lambda i, j: (0, j)),
            pl.BlockSpec((tn, d), lambda i, j: (j, 0)),
            pl.BlockSpec((1, d), lambda i, j: (0, 0)),
        ],
        out_specs=pl.BlockSpec((tm, d), lambda i, j: (i, 0)),
        out_shape=jax.ShapeDtypeStruct((m, d), _F32),
        scratch_shapes=[pltpu.VMEM((tm, d), _MXU_DTYPE)],
        compiler_params=_params("parallel", "arbitrary"),
        name="merge_step",
    )(x, norm_pre.reshape(1, d), w_gates, w_gates, w_gates, y_conv, y_attn, y_hgrn,
      wb_conv, wb_attn, wb_hgrn, w_o, norm_post.reshape(1, d))


def kernel(x, p, ffn1_norm_pre, ffn1_w_gu, ffn1_w_down, ffn1_norm_post, mix_norm_pre, w_in, conv_w, attn_sinks, hgrn_lb_logits, hgrn_norm, w_branch_conv, w_branch_attn, w_branch_hgrn, w_o, mix_norm_post, ffn2_norm_pre, ffn2_w_gu, ffn2_w_down, ffn2_norm_post, ple_norm_pre, w_ple_gate, w_ple_proj, ple_norm_post):
    batch, seq, d = x.shape
    depth = p.shape[0]
    m = batch * seq
    n_conv = 3 * CONV_WIDTH
    n_attn = (ATTN_HEADS + 2 * ATTN_KV_HEADS) * HEAD_DIM
    n_hgrn = 2 * HGRN_HEADS * (HGRN_DK + HGRN_DV)
    o_attn = n_conv
    o_hgrn = o_attn + n_attn
    o_gate = o_hgrn + n_hgrn

    xf = x.reshape(m, d)
    for l in range(depth):
        xf = _ffn_half_step(xf, ffn1_norm_pre[l], _to_mxu(ffn1_w_gu, l), _to_mxu(ffn1_w_down, l), ffn1_norm_post[l])
        w_conv, w_attn, w_hgrn, w_gates = _split_to_mxu(w_in, l, (n_conv, n_attn, n_hgrn, 3 * d))
        y_conv = _conv_branch(xf, mix_norm_pre[l], w_conv, conv_w[l], seq)
        y_attn = _attn_branch(xf, mix_norm_pre[l], w_attn, attn_sinks[l], seq)
        y_hgrn = _hgrn_branch(xf, mix_norm_pre[l], w_hgrn, hgrn_lb_logits, hgrn_norm[l], seq, l)
        xf = _merge_step(xf, mix_norm_pre[l], w_gates, y_conv, y_attn, y_hgrn,
                         _to_mxu(w_branch_conv, l), _to_mxu(w_branch_attn, l), _to_mxu(w_branch_hgrn, l),
                         _to_mxu(w_o, l), mix_norm_post[l])
        xf = _ffn_half_step(xf, ffn2_norm_pre[l], _to_mxu(ffn2_w_gu, l), _to_mxu(ffn2_w_down, l), ffn2_norm_post[l])
        xf = _ple_step(xf, p.reshape(depth, m, -1), l, ple_norm_pre[l], _to_mxu(w_ple_gate, l),
                       _to_mxu(w_ple_proj, l), ple_norm_post[l])
    return xf.reshape(batch, seq, d)
```

```python
import functools
import math

import jax
import jax.numpy as jnp
from jax import lax
from jax.experimental import pallas as pl
from jax.experimental.pallas import tpu as pltpu

_MXU_DTYPE = jnp.bfloat16
_F32 = jnp.float32

NORM_EPS = 1e-6
NEG_BIG = -1e30
LB_FLOOR = 1e-20

CONV_WIDTH = 512
CONV_K = 3
ATTN_HEADS = 16
ATTN_KV_HEADS = 2
HEAD_DIM = 64
ATTN_BLOCK = 128
HGRN_HEADS = 8
HGRN_DK = 128
HGRN_DV = 64

_LANES = 128
_LOG2E = math.log2(math.e)
_HALF = 64

_HGRN_CHUNK = 64
_HGRN_SAFE_LOG_DECAY = 80.0

_FFN_TM = 1024
_FFN_RB = 512
_FFN_TF = 512
_MIX_TM = 512
_MIX_TN = 512
_PLE_TM = 1024
_CONV_TM = 512
_ROW_BLOCK = 256
_CAST_ROWS = 512
_CAST_COLS = 2816
_SPLIT_ROWS = 128
_ATTN_TM = 512
_HGRN_TM = 512
_HGRN_UNROLL = 2
_ATTN_SKEW = 6
_ATTN_FILL_HEADS = (0, 5, 10)

_VMEM_LIMIT = 60 * 1024 * 1024


def _pick(n, pref):
    t = min(n, pref)
    assert n % t == 0, (n, pref)
    return t


def _rmsnorm(xf, w):
    return xf * lax.rsqrt(jnp.mean(xf * xf, axis=-1, keepdims=True) + NORM_EPS) * w


def _reduce_steps(x_ref, npre_ref, npost_ref, o_ref, h_ref, partial, *, n_steps, rb, rb_edge, scale):
    j = pl.program_id(1)
    tm = x_ref.shape[0]

    def run(first, last, rows_per_block):
        w_post = npost_ref[...] * scale
        for r in range(tm // rows_per_block):
            rows = slice(r * rows_per_block, (r + 1) * rows_per_block)
            if first:
                h = _rmsnorm(x_ref[rows, :], npre_ref[...]).astype(h_ref.dtype)
                h_ref[rows, :] = h
            else:
                h = h_ref[rows, :]
            acc = partial(h, rows)
            if not first:
                acc = o_ref[rows, :] + acc
            o_ref[rows, :] = x_ref[rows, :] + _rmsnorm(acc, w_post) if last else acc

    if n_steps == 1:
        run(True, True, rb_edge)
        return
    pl.when(j == 0)(functools.partial(run, True, False, rb_edge))
    if n_steps > 2:
        pl.when(jnp.logical_and(j > 0, j < n_steps - 1))(functools.partial(run, False, False, rb))
    pl.when(j == n_steps - 1)(functools.partial(run, False, True, rb_edge))


def _dot(a, b):
    return jnp.dot(a, b, preferred_element_type=_F32)


def _dot_nt(a, b):
    return lax.dot_general(a, b, (((1,), (1,)), ((), ())), preferred_element_type=_F32)


def _dot_tn(a, b):
    return lax.dot_general(a, b, (((0,), (0,)), ((), ())), preferred_element_type=_F32)


def _params(*sem):
    return pltpu.CompilerParams(dimension_semantics=sem, vmem_limit_bytes=_VMEM_LIMIT)


def _const_spec(shape):
    nd = len(shape)
    return pl.BlockSpec(shape, lambda *_: (0,) * nd, pipeline_mode=pl.Buffered(1))


def _cast_kernel(w_ref, o_ref):
    o_ref[...] = w_ref[...].astype(o_ref.dtype)


def _to_mxu(w, layer, col0=0, width=None):
    _, r, c = w.shape
    width = c if width is None else width
    span = math.gcd(col0, width)
    tc = max(t for t in range(_LANES, min(span, _CAST_COLS) + 1, _LANES) if span % t == 0)
    tr = _pick(r, _CAST_ROWS)
    return pl.pallas_call(
        _cast_kernel,
        grid=(r // tr, width // tc),
        in_specs=[pl.BlockSpec((None, tr, tc), lambda i, j: (layer, i, j + col0 // tc))],
        out_specs=pl.BlockSpec((tr, tc), lambda i, j: (i, j)),
        out_shape=jax.ShapeDtypeStruct((r, width), _MXU_DTYPE),
        compiler_params=_params("parallel", "parallel"),
        name="weight_cast",
    )(w)


def _split_cast_kernel(w_ref, *o_refs):
    col = 0
    for o_ref in o_refs:
        o_ref[...] = w_ref[:, col:col + o_ref.shape[1]].astype(o_ref.dtype)
        col += o_ref.shape[1]


def _split_to_mxu(w, layer, widths):
    _, r, c = w.shape
    assert sum(widths) == c and all(n % _LANES == 0 for n in widths)
    tr = _pick(r, _SPLIT_ROWS)
    return pl.pallas_call(
        _split_cast_kernel,
        grid=(r // tr,),
        in_specs=[pl.BlockSpec((None, tr, c), lambda i: (layer, i, 0))],
        out_specs=[pl.BlockSpec((tr, n), lambda i: (i, 0)) for n in widths],
        out_shape=[jax.ShapeDtypeStruct((r, n), _MXU_DTYPE) for n in widths],
        compiler_params=_params("parallel"),
        name="weight_split_cast",
    )(w)


def _ffn_kernel(x_ref, npre_ref, wg_ref, wu_ref, wd_ref, npost_ref, o_ref, h_ref, **steps):
    def partial(h, rows):
        g = _dot(h, wg_ref[...])
        u = _dot(h, wu_ref[...])
        a = (g * jax.nn.sigmoid(g) * u).astype(_MXU_DTYPE)
        return _dot(a, wd_ref[...])

    _reduce_steps(x_ref, npre_ref, npost_ref, o_ref, h_ref, partial, scale=0.5, **steps)


def _ffn_half_step(x, norm_pre, w_gu, w_down, norm_post):
    m, d = x.shape
    f = w_down.shape[0]
    tm, tf = _pick(m, _FFN_TM), _pick(f, _FFN_TF)
    nf = f // tf
    return pl.pallas_call(
        functools.partial(_ffn_kernel, n_steps=nf, rb=min(tm, _FFN_RB), rb_edge=min(tm, _FFN_RB)),
        grid=(m // tm, nf),
        in_specs=[
            pl.BlockSpec((tm, d), lambda i, j: (i, 0)),
            pl.BlockSpec((1, d), lambda i, j: (0, 0)),
            pl.BlockSpec((d, tf), lambda i, j: (0, j)),
            pl.BlockSpec((d, tf), lambda i, j: (0, j + nf)),
            pl.BlockSpec((tf, d), lambda i, j: (j, 0)),
            pl.BlockSpec((1, d), lambda i, j: (0, 0)),
        ],
        out_specs=pl.BlockSpec((tm, d), lambda i, j: (i, 0)),
        out_shape=jax.ShapeDtypeStruct((m, d), _F32),
        scratch_shapes=[pltpu.VMEM((tm, d), _MXU_DTYPE)],
        compiler_params=_params("parallel", "arbitrary"),
        name="ffn_half_step",
    )(x, norm_pre.reshape(1, d), w_gu, w_gu, w_down, norm_post.reshape(1, d))


def _ple_kernel(x_ref, p_ref, npre_ref, wg_ref, wp_ref, npost_ref, o_ref, *, rb):
    for r in range(x_ref.shape[0] // rb):
        rows = slice(r * rb, (r + 1) * rb)
        x = x_ref[rows, :]
        hp = _rmsnorm(x, npre_ref[...]).astype(_MXU_DTYPE)
        gate = jax.nn.sigmoid(_dot(hp, wg_ref[...]))
        proj = _dot(p_ref[rows, :].astype(_MXU_DTYPE), wp_ref[...])
        o_ref[rows, :] = x + _rmsnorm(gate * proj, npost_ref[...])


def _ple_step(x, p, layer, norm_pre, w_gate, w_proj, norm_post):
    m, d = x.shape
    dp = p.shape[2]
    tm = _pick(m, _PLE_TM)
    return pl.pallas_call(
        functools.partial(_ple_kernel, rb=min(tm, _ROW_BLOCK)),
        grid=(m // tm,),
        in_specs=[
            pl.BlockSpec((tm, d), lambda i: (i, 0)),
            pl.BlockSpec((None, tm, dp), lambda i: (layer, i, 0)),
            _const_spec((1, d)),
            _const_spec((d, d)),
            _const_spec((dp, d)),
            _const_spec((1, d)),
        ],
        out_specs=pl.BlockSpec((tm, d), lambda i: (i, 0)),
        out_shape=jax.ShapeDtypeStruct((m, d), _F32),
        compiler_params=_params("parallel"),
        name="ple_step",
    )(x, p, norm_pre.reshape(1, d), w_gate, w_proj, norm_post.reshape(1, d))


def _conv_kernel(x_ref, npre_ref, w_ref, cw_ref, o_ref, carry_ref, *, seq_tiles, rb):
    first = (pl.program_id(0) % seq_tiles) == 0
    tm = x_ref.shape[0]
    cwid = o_ref.shape[1]

    @pl.when(first)
    def _():
        carry_ref[...] = jnp.zeros_like(carry_ref)

    cw = cw_ref[...]
    row = lax.broadcasted_iota(jnp.int32, (rb, cwid), 0)
    tail = carry_ref[...]
    for r in range(tm // rb):
        rows = slice(r * rb, (r + 1) * rb)
        h = _rmsnorm(x_ref[rows, :], npre_ref[...]).astype(_MXU_DTYPE)
        proj = _dot(h, w_ref[...])
        xin, b_gate, c_gate = proj[:, :cwid], proj[:, cwid:2 * cwid], proj[:, 2 * cwid:]
        u = c_gate * xin
        prev1 = tail[7:8, :]
        prev2 = tail[6:7, :]
        u1 = jnp.where(row == 0, prev1, pltpu.roll(u, 1, 0))
        u2 = jnp.where(row == 0, prev2, jnp.where(row == 1, prev1, pltpu.roll(u, 2, 0)))
        y = cw[2:3, :] * u + cw[1:2, :] * u1 + cw[0:1, :] * u2
        o_ref[rows, :] = (b_gate * y).astype(o_ref.dtype)
        tail = u[rb - 8:, :]
    carry_ref[...] = tail


def _conv_branch(x, norm_pre, w_in_conv, conv_w, seq):
    m, d = x.shape
    cwid = conv_w.shape[1]
    tm = _pick(seq, _CONV_TM)
    cw = jnp.zeros((8, cwid), _F32).at[:CONV_K].set(conv_w)
    return pl.pallas_call(
        functools.partial(_conv_kernel, seq_tiles=seq // tm, rb=min(tm, _ROW_BLOCK)),
        grid=(m // tm,),
        in_specs=[
            pl.BlockSpec((tm, d), lambda i: (i, 0)),
            _const_spec((1, d)),
            _const_spec((d, 3 * cwid)),
            _const_spec((8, cwid)),
        ],
        out_specs=pl.BlockSpec((tm, cwid), lambda i: (i, 0)),
        out_shape=jax.ShapeDtypeStruct((m, cwid), _MXU_DTYPE),
        scratch_shapes=[pltpu.VMEM((8, cwid), _F32)],
        compiler_params=_params("arbitrary"),
        name="conv_branch",
    )(x, norm_pre.reshape(1, d), w_in_conv, cw)


def _attn_kernel(sink_ref, x_ref, npre_ref, w_ref, o_ref, q_s, k_s, v_s, bias_s, *, seq_tiles):
    blk = ATTN_BLOCK
    tm = x_ref.shape[0]
    nq = ATTN_HEADS * HEAD_DIM
    nkv = ATTN_KV_HEADS * HEAD_DIM
    first = (pl.program_id(0) % seq_tiles) == 0
    group = ATTN_HEADS // ATTN_KV_HEADS

    @pl.when(pl.program_id(0) == 0)
    def _():
        qi = lax.broadcasted_iota(jnp.int32, (blk, 2 * blk), 0)
        ki = lax.broadcasted_iota(jnp.int32, (blk, 2 * blk), 1)
        dist = qi + blk - ki
        band = (dist >= 0) & (dist < blk)
        distf = dist.astype(_F32)
        for head in range(ATTN_HEADS):
            slope = _LOG2E * 2.0 ** (-8.0 * (head + 1) / ATTN_HEADS)
            bias_s[0, head] = jnp.where(band, -slope * distf, NEG_BIG)
            bias_s[1, head] = jnp.where(band & (ki >= blk), -slope * distf, NEG_BIG)

    @pl.when(first)
    def _():
        k_s[0:blk, :] = jnp.zeros((blk, nkv), _F32)
        v_s[0:blk, :] = jnp.zeros((blk, nkv), _F32)

    def block_norm(b):
        return _rmsnorm(x_ref[b * blk:(b + 1) * blk, :], npre_ref[...]).astype(_MXU_DTYPE)

    def project(hb, b, part):
        rows = slice(b * blk, (b + 1) * blk)
        if part < 2:
            cols = slice(part * (nq // 2), (part + 1) * (nq // 2))
            q_s[rows, cols] = (_dot(hb, w_ref[:, cols]) * (_LOG2E * HEAD_DIM ** -0.5)).astype(q_s.dtype)
        else:
            kv = _dot(hb, w_ref[:, nq:nq + 2 * nkv])
            k_s[blk + b * blk:blk + (b + 1) * blk, :] = kv[:, :nkv]
            v_s[blk + b * blk:blk + (b + 1) * blk, :] = kv[:, nkv:]

    lo = lax.broadcasted_iota(jnp.int32, (1, _LANES), 1) < _HALF

    def block(b, fills):
        r0 = b * blk
        kc = k_s[r0:r0 + 2 * blk, :]
        vc = v_s[r0:r0 + 2 * blk, :]
        kr = pltpu.roll(kc, _HALF, 1)
        vr = pltpu.roll(vc, _HALF, 1)
        k_dup = [jnp.where(lo, kc, kr).astype(_MXU_DTYPE), jnp.where(lo, kr, kc).astype(_MXU_DTYPE)]
        v_half = [
            [jnp.where(lo, vc, 0.0).astype(_MXU_DTYPE), jnp.where(lo, 0.0, vr).astype(_MXU_DTYPE)],
            [jnp.where(lo, vr, 0.0).astype(_MXU_DTYPE), jnp.where(lo, 0.0, vc).astype(_MXU_DTYPE)],
        ]
        no_prev = first.astype(jnp.int32) if b == 0 else 0

        def scores(head):
            pair, half = divmod(head, 2)
            qp = q_s[r0:r0 + blk, pair * _LANES:(pair + 1) * _LANES]
            zero = jnp.zeros_like(qp)
            qm = jnp.where(lo, qp, zero) if half == 0 else jnp.where(lo, zero, qp)
            s = _dot_nt(qm, k_dup[head // group]) + bias_s[no_prev, head]
            sink = sink_ref[head] * _LOG2E
            mx = jnp.maximum(jnp.max(s, axis=-1, keepdims=True), sink)
            pr = jnp.exp2(s - mx)
            denom = jnp.sum(pr, axis=-1, keepdims=True) + jnp.exp2(sink - mx)
            return pr.astype(_MXU_DTYPE), denom

        acc = {}

        def values(head, pr, denom):
            pair, half = divmod(head, 2)
            o = _dot(pr, v_half[head // group][half]) / denom
            if half == 0:
                acc[pair] = o
            else:
                o_ref[r0:r0 + blk, pair * _LANES:(pair + 1) * _LANES] = (acc.pop(pair) + o).astype(o_ref.dtype)

        pending = []
        for head in range(ATTN_HEADS):
            if head in fills:
                fills[head]()
            pending.append((head,) + scores(head))
            if len(pending) > _ATTN_SKEW:
                values(*pending.pop(0))
        while pending:
            values(*pending.pop(0))

    n_blk = tm // blk
    hb = block_norm(0)
    for part in range(3):
        project(hb, 0, part)
    for b in range(n_blk):
        fills = {}
        if b + 1 < n_blk:
            hb = block_norm(b + 1)
            fills = {head: functools.partial(project, hb, b + 1, part)
                     for part, head in enumerate(_ATTN_FILL_HEADS)}
        block(b, fills)
    k_s[0:blk, :] = k_s[tm:tm + blk, :]
    v_s[0:blk, :] = v_s[tm:tm + blk, :]


def _attn_branch(x, norm_pre, w_in_attn, sinks, seq):
    m, d = x.shape
    nq = ATTN_HEADS * HEAD_DIM
    nkv = ATTN_KV_HEADS * HEAD_DIM
    tm = _pick(seq, _ATTN_TM)
    assert tm % ATTN_BLOCK == 0 and nkv == _LANES
    return pl.pallas_call(
        functools.partial(_attn_kernel, seq_tiles=seq // tm),
        grid=(m // tm,),
        in_specs=[
            pl.BlockSpec(memory_space=pltpu.SMEM),
            pl.BlockSpec((tm, d), lambda i: (i, 0)),
            _const_spec((1, d)),
            _const_spec((d, nq + 2 * nkv)),
        ],
        out_specs=pl.BlockSpec((tm, nq), lambda i: (i, 0)),
        out_shape=jax.ShapeDtypeStruct((m, nq), _MXU_DTYPE),
        scratch_shapes=[
            pltpu.VMEM((tm, nq), _MXU_DTYPE),
            pltpu.VMEM((tm + ATTN_BLOCK, nkv), _F32),
            pltpu.VMEM((tm + ATTN_BLOCK, nkv), _F32),
            pltpu.VMEM((2, ATTN_HEADS, ATTN_BLOCK, 2 * ATTN_BLOCK), _F32),
        ],
        compiler_params=_params("arbitrary"),
        name="attn_branch",
    )(sinks, x, norm_pre.reshape(1, d), w_in_attn)


def _hgrn_kernel(x_ref, npre_ref, w_ref, lbl_ref, nw_ref, o_ref, pr_s, st_s, bak_s, *, seq_tiles, layer):
    ch = _HGRN_CHUNK
    tm = x_ref.shape[0]
    nk = HGRN_HEADS * HGRN_DK
    nv = HGRN_HEADS * HGRN_DV
    first = (pl.program_id(0) % seq_tiles) == 0

    @pl.when(first)
    def _():
        st_s[...] = jnp.zeros_like(st_s)

    lg = lbl_ref[...]
    e = jnp.exp(lg - jnp.max(lg, axis=0, keepdims=True))
    sm = e / jnp.sum(e, axis=0, keepdims=True)
    lb = jnp.zeros((1, nk), _F32)
    for prev in range(layer):
        lb = lb + sm[prev:prev + 1, :]
    lb_floor = jnp.maximum(lb, LB_FLOOR)
    one_minus_lb = 1.0 - lb

    lo = lax.broadcasted_iota(jnp.int32, (1, _LANES), 1) < _HALF
    rid = lax.broadcasted_iota(jnp.int32, (ch, nk), 0)
    tril = lax.broadcasted_iota(jnp.int32, (ch, ch), 0) >= lax.broadcasted_iota(jnp.int32, (ch, ch), 1)
    nw = nw_ref[...]

    def chunk(c, a_min, *, exact, fill=(None, None)):
        rows = slice(c * ch, (c + 1) * ch) if isinstance(c, int) else pl.ds(pl.multiple_of(c * ch, ch), ch)
        if fill[0] is not None:
            fill[0]()
        q = pr_s[rows, 0:nk]
        z = pr_s[rows, nk:2 * nk]
        qs = q * jax.nn.sigmoid(q)
        sig_pos = jax.nn.sigmoid(z)
        sig_neg = 1.0 - sig_pos
        log_f = jnp.log(sig_pos + lb_floor * sig_neg)
        kk = one_minus_lb * sig_neg
        a = log_f
        step = 1
        while step < ch:
            a = a + jnp.where(rid >= step, pltpu.roll(a, step, 0), 0.0)
            step *= 2
        a_last = a[ch - 1:ch, :]
        q_dec = (qs * jnp.exp(a)).astype(_MXU_DTYPE)
        k_end = (kk * jnp.exp(a_last - a)).astype(_MXU_DTYPE)
        d_end = jnp.exp(a_last)
        heads = range(HGRN_HEADS)
        sls = [slice(hd * HGRN_DK, (hd + 1) * HGRN_DK) for hd in heads]
        vps = [pr_s[rows, 2 * nk + pr * _LANES:2 * nk + (pr + 1) * _LANES] for pr in range(HGRN_HEADS // 2)]
        vpm = [v.astype(_MXU_DTYPE) for v in vps]
        if exact:
            sid = lax.broadcasted_iota(jnp.int32, (ch, nk), 0)
            sid_v = lax.broadcasted_iota(jnp.int32, (ch, _LANES), 0)

            def query_row(t, intra):
                a_t = jnp.sum(jnp.where(sid == t, a, 0.0), axis=0, keepdims=True)
                q_t = jnp.sum(jnp.where(sid == t, qs, 0.0), axis=0, keepdims=True)
                wts = q_t * jnp.exp(jnp.where(sid <= t, a_t - a, NEG_BIG)) * kk
                out = []
                for hd in heads:
                    col = jnp.sum(wts[:, sls[hd]], axis=-1, keepdims=True)
                    o_t = jnp.sum(col * vps[hd // 2], axis=0, keepdims=True)
                    out.append(jnp.where(sid_v == t, o_t, intra[hd]))
                return tuple(out)

            intra = lax.fori_loop(0, ch, query_row, tuple(jnp.zeros((ch, _LANES), _F32) for _ in heads))
        else:
            k_inv = (kk * jnp.exp(-a)).astype(_MXU_DTYPE)
            att = [_dot_nt(q_dec[:, sls[hd]], k_inv[:, sls[hd]]) for hd in heads]
        upd = [_dot_tn(vpm[hd // 2], k_end[:, sls[hd]]) for hd in heads]
        states = [st_s[hd] for hd in heads]
        inter = [_dot_nt(q_dec[:, sls[hd]], states[hd].astype(_MXU_DTYPE)) for hd in heads]
        if fill[1] is not None:
            fill[1]()
        if not exact:
            intra = [_dot(jnp.where(tril, att[hd], 0.0).astype(_MXU_DTYPE), vpm[hd // 2]) for hd in heads]
        for hd in heads:
            st_s[hd] = states[hd] * d_end[:, sls[hd]] + upd[hd]
        for pair in range(HGRN_HEADS // 2):
            gp = pr_s[rows, 2 * nk + nv + pair * _LANES:2 * nk + nv + (pair + 1) * _LANES]
            outs = [inter[2 * pair + half] + intra[2 * pair + half] for half in range(2)]
            o = jnp.where(lo, outs[0], outs[1])
            osq = o * o
            ms_lo = jnp.sum(jnp.where(lo, osq, 0.0), axis=-1, keepdims=True) * (1.0 / HGRN_DV)
            ms_hi = jnp.sum(jnp.where(lo, 0.0, osq), axis=-1, keepdims=True) * (1.0 / HGRN_DV)
            rinv = jnp.where(lo, lax.rsqrt(ms_lo + NORM_EPS), lax.rsqrt(ms_hi + NORM_EPS))
            on = o * rinv * nw[:, pair * _LANES:(pair + 1) * _LANES]
            o_ref[rows, pair * _LANES:(pair + 1) * _LANES] = (on * (gp * jax.nn.sigmoid(gp))).astype(o_ref.dtype)
        return jnp.minimum(a_min, a_last)

    bak_s[...] = st_s[...]

    rg = 2 * ch
    n_rg = tm // rg
    n_sl = 4
    scols = (2 * nk + 2 * nv) // n_sl

    def group_norm(g):
        return _rmsnorm(x_ref[g * rg:(g + 1) * rg, :], npre_ref[...]).astype(_MXU_DTYPE)

    def project(hg, g, s):
        cols = slice(s * scols, (s + 1) * scols)
        pr_s[g * rg:(g + 1) * rg, cols] = _dot(hg, w_ref[:, cols])

    hg = group_norm(0)
    for s in range(n_sl):
        project(hg, 0, s)
    a_min = jnp.zeros((1, nk), _F32)
    for c in range(tm // ch):
        g, part = divmod(c, 2)
        fill = (None, None)
        if g + 1 < n_rg:
            if part == 0:
                hg = group_norm(g + 1)
            fill = tuple(functools.partial(project, hg, g + 1, 2 * part + k) for k in range(2))
        a_min = chunk(c, a_min, exact=False, fill=fill)

    @pl.when(jnp.min(a_min) < -_HGRN_SAFE_LOG_DECAY)
    def _():
        st_s[...] = bak_s[...]
        lax.fori_loop(0, tm // ch, functools.partial(chunk, exact=True), jnp.zeros((1, nk), _F32))


def _hgrn_branch(x, norm_pre, w_in_hgrn, lb_logits, norm_w, seq, layer):
    m, d = x.shape
    nk = HGRN_HEADS * HGRN_DK
    nv = HGRN_HEADS * HGRN_DV
    depth = lb_logits.shape[0]
    tm = _pick(seq, _HGRN_TM)
    assert HGRN_DK == _LANES and 2 * HGRN_DV == _LANES and tm % _HGRN_CHUNK == 0
    nw = jnp.tile(norm_w.reshape(1, HGRN_DV), (1, HGRN_HEADS))
    return pl.pallas_call(
        functools.partial(_hgrn_kernel, seq_tiles=seq // tm, layer=layer),
        grid=(m // tm,),
        in_specs=[
            pl.BlockSpec((tm, d), lambda i: (i, 0)),
            _const_spec((1, d)),
            _const_spec((d, 2 * nk + 2 * nv)),
            _const_spec((depth, nk)),
            _const_spec((1, nv)),
        ],
        out_specs=pl.BlockSpec((tm, nv), lambda i: (i, 0)),
        out_shape=jax.ShapeDtypeStruct((m, nv), _MXU_DTYPE),
        scratch_shapes=[
            pltpu.VMEM((tm, 2 * nk + 2 * nv), _F32),
            pltpu.VMEM((HGRN_HEADS, _LANES, HGRN_DK), _F32),
            pltpu.VMEM((HGRN_HEADS, _LANES, HGRN_DK), _F32),
        ],
        compiler_params=_params("arbitrary"),
        name="hgrn_branch",
    )(x, norm_pre.reshape(1, d), w_in_hgrn, lb_logits, nw)


def _merge_kernel(x_ref, npre_ref, wgc_ref, wga_ref, wgh_ref, yc_ref, ya_ref, yh_ref,
                  wbc_ref, wba_ref, wbh_ref, wo_ref, npost_ref, o_ref, h_ref, **steps):
    def partial(h, rows):
        merged = (jax.nn.sigmoid(_dot(h, wgc_ref[...])) * _dot(yc_ref[rows, :], wbc_ref[...])
                  + jax.nn.sigmoid(_dot(h, wga_ref[...])) * _dot(ya_ref[rows, :], wba_ref[...])
                  + jax.nn.sigmoid(_dot(h, wgh_ref[...])) * _dot(yh_ref[rows, :], wbh_ref[...]))
        return _dot(merged.astype(_MXU_DTYPE), wo_ref[...])

    _reduce_steps(x_ref, npre_ref, npost_ref, o_ref, h_ref, partial, scale=1.0, **steps)


def _merge_step(x, norm_pre, w_gates, y_conv, y_attn, y_hgrn, wb_conv, wb_attn, wb_hgrn, w_o, norm_post):
    m, d = x.shape
    tm, tn = _pick(m, _MIX_TM), _pick(d, _MIX_TN)
    nt = d // tn
    nc, na, nh = y_conv.shape[1], y_attn.shape[1], y_hgrn.shape[1]
    return pl.pallas_call(
        functools.partial(_merge_kernel, n_steps=nt, rb=min(tm, _FFN_RB), rb_edge=min(tm, _ROW_BLOCK)),
        grid=(m // tm, nt),
        in_specs=[
            pl.BlockSpec((tm, d), lambda i, j: (i, 0)),
            pl.BlockSpec((1, d), lambda i, j: (0, 0)),
            pl.BlockSpec((d, tn), lambda i, j: (0, j)),
            pl.BlockSpec((d, tn), lambda i, j: (0, j + nt)),
            pl.BlockSpec((d, tn), lambda i, j: (0, j + 2 * nt)),
            pl.BlockSpec((tm, nc), lambda i, j: (i, 0)),
            pl.BlockSpec((tm, na), lambda i, j: (i, 0)),
            pl.BlockSpec((tm, nh), lambda i, j: (i, 0)),
            pl.BlockSpec((nc, tn), lambda i, j: (0, j)),
            pl.BlockSpec((na, tn), lambda i, j: (0, j)),
            pl.BlockSpec((nh, tn), lambda i, j: (0, j)),
            pl.BlockSpec((tn, d), lambda i, j: (j, 0)),
            pl.BlockSpec((1, d), lambda i, j: (0, 0)),
        ],
        out_specs=pl.BlockSpec((tm, d), lambda i, j: (i, 0)),
        out_shape=jax.ShapeDtypeStruct((m, d), _F32),
        scratch_shapes=[pltpu.VMEM((tm, d), _MXU_DTYPE)],
        compiler_params=_params("parallel", "arbitrary"),
        name="merge_step",
    )(x, norm_pre.reshape(1, d), w_gates, w_gates, w_gates, y_conv, y_attn, y_hgrn,
      wb_conv, wb_attn, wb_hgrn, w_o, norm_post.reshape(1, d))


def kernel(x, p, ffn1_norm_pre, ffn1_w_gu, ffn1_w_down, ffn1_norm_post, mix_norm_pre, w_in, conv_w, attn_sinks, hgrn_lb_logits, hgrn_norm, w_branch_conv, w_branch_attn, w_branch_hgrn, w_o, mix_norm_post, ffn2_norm_pre, ffn2_w_gu, ffn2_w_down, ffn2_norm_post, ple_norm_pre, w_ple_gate, w_ple_proj, ple_norm_post):
    batch, seq, d = x.shape
    depth = p.shape[0]
    m = batch * seq
    n_conv = 3 * CONV_WIDTH
    n_attn = (ATTN_HEADS + 2 * ATTN_KV_HEADS) * HEAD_DIM
    n_hgrn = 2 * HGRN_HEADS * (HGRN_DK + HGRN_DV)
    o_attn = n_conv
    o_hgrn = o_attn + n_attn
    o_gate = o_hgrn + n_hgrn

    xf = x.reshape(m, d)
    for l in range(depth):
        xf = _ffn_half_step(xf, ffn1_norm_pre[l], _to_mxu(ffn1_w_gu, l), _to_mxu(ffn1_w_down, l), ffn1_norm_post[l])
        w_conv, w_attn, w_hgrn, w_gates = _split_to_mxu(w_in, l, (n_conv, n_attn, n_hgrn, 3 * d))
        y_conv = _conv_branch(xf, mix_norm_pre[l], w_conv, conv_w[l], seq)
        y_attn = _attn_branch(xf, mix_norm_pre[l], w_attn, attn_sinks[l], seq)
        y_hgrn = _hgrn_branch(xf, mix_norm_pre[l], w_hgrn, hgrn_lb_logits, hgrn_norm[l], seq, l)
        xf = _merge_step(xf, mix_norm_pre[l], w_gates, y_conv, y_attn, y_hgrn,
                         _to_mxu(w_branch_conv, l), _to_mxu(w_branch_attn, l), _to_mxu(w_branch_hgrn, l),
                         _to_mxu(w_o, l), mix_norm_post[l])
        xf = _ffn_half_step(xf, ffn2_norm_pre[l], _to_mxu(ffn2_w_gu, l), _to_mxu(ffn2_w_down, l), ffn2_norm_post[l])
        xf = _ple_step(xf, p.reshape(depth, m, -1), l, ple_norm_pre[l], _to_mxu(w_ple_gate, l),
                       _to_mxu(w_ple_proj, l), ple_norm_post[l])
    return xf.reshape(batch, seq, d)
```

```python
import functools
import math

import jax
import jax.numpy as jnp
from jax import lax
from jax.experimental import pallas as pl
from jax.experimental.pallas import tpu as pltpu

_MXU_DTYPE = jnp.bfloat16
_F32 = jnp.float32

NORM_EPS = 1e-6
NEG_BIG = -1e30
LB_FLOOR = 1e-20

CONV_WIDTH = 512
CONV_K = 3
ATTN_HEADS = 16
ATTN_KV_HEADS = 2
HEAD_DIM = 64
ATTN_BLOCK = 128
HGRN_HEADS = 8
HGRN_DK = 128
HGRN_DV = 64

_LANES = 128
_HALF = 64
_LOG2E = math.log2(math.e)

_HGRN_CHUNK = 64
_HGRN_SAFE_LOG_DECAY = 80.0

_FFN_TM = 1024
_FFN_RB = 512
_FFN_TF = 512
_MIX_TM = 512
_MIX_TN = 512
_PLE_TM = 1024
_CONV_TM = 512
_ROW_BLOCK = 256
_CAST_ROWS = 512
_CAST_COLS = 2816
_SPLIT_ROWS = 128
_ATTN_TM = 512
_HGRN_TM = 512
_ATTN_SKEW = 6
_ATTN_FILL_HEADS = (0, 5, 10)

_VMEM_LIMIT = 60 * 1024 * 1024


def _pick(n, pref):
    t = min(n, pref)
    assert n % t == 0, (n, pref)
    return t


def _rmsnorm(xf, w):
    return xf * lax.rsqrt(jnp.mean(xf * xf, axis=-1, keepdims=True) + NORM_EPS) * w


def _reduce_steps(x_ref, npre_ref, npost_ref, o_ref, h_ref, partial, *, n_steps, rb, rb_edge, scale):
    j = pl.program_id(1)
    tm = x_ref.shape[0]

    def run(first, last, rows_per_block):
        w_post = npost_ref[...] * scale
        for r in range(tm // rows_per_block):
            rows = slice(r * rows_per_block, (r + 1) * rows_per_block)
            if first:
                h = _rmsnorm(x_ref[rows, :], npre_ref[...]).astype(h_ref.dtype)
                h_ref[rows, :] = h
            else:
                h = h_ref[rows, :]
            acc = partial(h, rows)
            if not first:
                acc = o_ref[rows, :] + acc
            o_ref[rows, :] = x_ref[rows, :] + _rmsnorm(acc, w_post) if last else acc

    if n_steps == 1:
        run(True, True, rb_edge)
        return
    pl.when(j == 0)(functools.partial(run, True, False, rb_edge))
    if n_steps > 2:
        pl.when(jnp.logical_and(j > 0, j < n_steps - 1))(functools.partial(run, False, False, rb))
    pl.when(j == n_steps - 1)(functools.partial(run, False, True, rb_edge))


def _dot(a, b):
    return jnp.dot(a, b, preferred_element_type=_F32)


def _dot_nt(a, b):
    return lax.dot_general(a, b, (((1,), (1,)), ((), ())), preferred_element_type=_F32)


def _dot_tn(a, b):
    return lax.dot_general(a, b, (((0,), (0,)), ((), ())), preferred_element_type=_F32)


def _params(*sem):
    return pltpu.CompilerParams(dimension_semantics=sem, vmem_limit_bytes=_VMEM_LIMIT)


def _const_spec(shape):
    nd = len(shape)
    return pl.BlockSpec(shape, lambda *_: (0,) * nd, pipeline_mode=pl.Buffered(1))


def _cast_kernel(w_ref, o_ref):
    o_ref[...] = w_ref[...].astype(o_ref.dtype)


def _to_mxu(w, layer):
    _, r, c = w.shape
    tc = max(t for t in range(_LANES, min(c, _CAST_COLS) + 1, _LANES) if c % t == 0)
    tr = _pick(r, _CAST_ROWS)
    return pl.pallas_call(
        _cast_kernel,
        grid=(r // tr, c // tc),
        in_specs=[pl.BlockSpec((None, tr, tc), lambda i, j: (layer, i, j))],
        out_specs=pl.BlockSpec((tr, tc), lambda i, j: (i, j)),
        out_shape=jax.ShapeDtypeStruct((r, c), _MXU_DTYPE),
        compiler_params=_params("parallel", "parallel"),
        name="weight_cast",
    )(w)


def _split_cast_kernel(w_ref, *o_refs):
    col = 0
    for o_ref in o_refs:
        o_ref[...] = w_ref[:, col:col + o_ref.shape[1]].astype(o_ref.dtype)
        col += o_ref.shape[1]


def _split_to_mxu(w, layer, widths):
    _, r, c = w.shape
    assert sum(widths) == c and all(n % _LANES == 0 for n in widths)
    tr = _pick(r, _SPLIT_ROWS)
    return pl.pallas_call(
        _split_cast_kernel,
        grid=(r // tr,),
        in_specs=[pl.BlockSpec((None, tr, c), lambda i: (layer, i, 0))],
        out_specs=[pl.BlockSpec((tr, n), lambda i: (i, 0)) for n in widths],
        out_shape=[jax.ShapeDtypeStruct((r, n), _MXU_DTYPE) for n in widths],
        compiler_params=_params("parallel"),
        name="weight_split_cast",
    )(w)


def _ffn_kernel(x_ref, npre_ref, wg_ref, wu_ref, wd_ref, npost_ref, o_ref, h_ref, **steps):
    def partial(h, rows):
        g = _dot(h, wg_ref[...])
        u = _dot(h, wu_ref[...])
        a = (g * jax.nn.sigmoid(g) * u).astype(_MXU_DTYPE)
        return _dot(a, wd_ref[...])

    _reduce_steps(x_ref, npre_ref, npost_ref, o_ref, h_ref, partial, scale=0.5, **steps)


def _ffn_half_step(x, norm_pre, w_gu, w_down, norm_post):
    m, d = x.shape
    f = w_down.shape[0]
    tm, tf = _pick(m, _FFN_TM), _pick(f, _FFN_TF)
    nf = f // tf
    return pl.pallas_call(
        functools.partial(_ffn_kernel, n_steps=nf, rb=min(tm, _FFN_RB), rb_edge=min(tm, _FFN_RB)),
        grid=(m // tm, nf),
        in_specs=[
            pl.BlockSpec((tm, d), lambda i, j: (i, 0)),
            pl.BlockSpec((1, d), lambda i, j: (0, 0)),
            pl.BlockSpec((d, tf), lambda i, j: (0, j)),
            pl.BlockSpec((d, tf), lambda i, j: (0, j + nf)),
            pl.BlockSpec((tf, d), lambda i, j: (j, 0)),
            pl.BlockSpec((1, d), lambda i, j: (0, 0)),
        ],
        out_specs=pl.BlockSpec((tm, d), lambda i, j: (i, 0)),
        out_shape=jax.ShapeDtypeStruct((m, d), _F32),
        scratch_shapes=[pltpu.VMEM((tm, d), _MXU_DTYPE)],
        compiler_params=_params("parallel", "arbitrary"),
        name="ffn_half_step",
    )(x, norm_pre.reshape(1, d), w_gu, w_gu, w_down, norm_post.reshape(1, d))


def _ple_kernel(x_ref, p_ref, npre_ref, wg_ref, wp_ref, npost_ref, o_ref, *, rb):
    for r in range(x_ref.shape[0] // rb):
        rows = slice(r * rb, (r + 1) * rb)
        x = x_ref[rows, :]
        hp = _rmsnorm(x, npre_ref[...]).astype(_MXU_DTYPE)
        gate = jax.nn.sigmoid(_dot(hp, wg_ref[...]))
        proj = _dot(p_ref[rows, :].astype(_MXU_DTYPE), wp_ref[...])
        o_ref[rows, :] = x + _rmsnorm(gate * proj, npost_ref[...])


def _ple_step(x, p, layer, norm_pre, w_gate, w_proj, norm_post):
    m, d = x.shape
    dp = p.shape[2]
    tm = _pick(m, _PLE_TM)
    return pl.pallas_call(
        functools.partial(_ple_kernel, rb=min(tm, _ROW_BLOCK)),
        grid=(m // tm,),
        in_specs=[
            pl.BlockSpec((tm, d), lambda i: (i, 0)),
            pl.BlockSpec((None, tm, dp), lambda i: (layer, i, 0)),
            _const_spec((1, d)),
            _const_spec((d, d)),
            _const_spec((dp, d)),
            _const_spec((1, d)),
        ],
        out_specs=pl.BlockSpec((tm, d), lambda i: (i, 0)),
        out_shape=jax.ShapeDtypeStruct((m, d), _F32),
        compiler_params=_params("parallel"),
        name="ple_step",
    )(x, p, norm_pre.reshape(1, d), w_gate, w_proj, norm_post.reshape(1, d))


def _conv_kernel(x_ref, npre_ref, w_ref, cw_ref, o_ref, carry_ref, *, seq_tiles, rb):
    first = (pl.program_id(0) % seq_tiles) == 0
    tm = x_ref.shape[0]
    cwid = o_ref.shape[1]

    @pl.when(first)
    def _():
        carry_ref[...] = jnp.zeros_like(carry_ref)

    cw = cw_ref[...]
    row = lax.broadcasted_iota(jnp.int32, (rb, cwid), 0)
    tail = carry_ref[...]
    for r in range(tm // rb):
        rows = slice(r * rb, (r + 1) * rb)
        h = _rmsnorm(x_ref[rows, :], npre_ref[...]).astype(_MXU_DTYPE)
        proj = _dot(h, w_ref[...])
        xin, b_gate, c_gate = proj[:, :cwid], proj[:, cwid:2 * cwid], proj[:, 2 * cwid:]
        u = c_gate * xin
        prev1 = tail[7:8, :]
        prev2 = tail[6:7, :]
        u1 = jnp.where(row == 0, prev1, pltpu.roll(u, 1, 0))
        u2 = jnp.where(row == 0, prev2, jnp.where(row == 1, prev1, pltpu.roll(u, 2, 0)))
        y = cw[2:3, :] * u + cw[1:2, :] * u1 + cw[0:1, :] * u2
        o_ref[rows, :] = (b_gate * y).astype(o_ref.dtype)
        tail = u[rb - 8:, :]
    carry_ref[...] = tail


def _conv_branch(x, norm_pre, w_in_conv, conv_w, seq):
    m, d = x.shape
    cwid = conv_w.shape[1]
    tm = _pick(seq, _CONV_TM)
    cw = jnp.zeros((8, cwid), _F32).at[:CONV_K].set(conv_w)
    return pl.pallas_call(
        functools.partial(_conv_kernel, seq_tiles=seq // tm, rb=min(tm, _ROW_BLOCK)),
        grid=(m // tm,),
        in_specs=[
            pl.BlockSpec((tm, d), lambda i: (i, 0)),
            _const_spec((1, d)),
            _const_spec((d, 3 * cwid)),
            _const_spec((8, cwid)),
        ],
        out_specs=pl.BlockSpec((tm, cwid), lambda i: (i, 0)),
        out_shape=jax.ShapeDtypeStruct((m, cwid), _MXU_DTYPE),
        scratch_shapes=[pltpu.VMEM((8, cwid), _F32)],
        compiler_params=_params("arbitrary"),
        name="conv_branch",
    )(x, norm_pre.reshape(1, d), w_in_conv, cw)


def _attn_kernel(sink_ref, x_ref, npre_ref, w_ref, o_ref, q_s, k_s, v_s, bias_s, *, seq_tiles):
    blk = ATTN_BLOCK
    tm = x_ref.shape[0]
    nq = ATTN_HEADS * HEAD_DIM
    nkv = ATTN_KV_HEADS * HEAD_DIM
    first = (pl.program_id(0) % seq_tiles) == 0
    group = ATTN_HEADS // ATTN_KV_HEADS

    @pl.when(pl.program_id(0) == 0)
    def _():
        qi = lax.broadcasted_iota(jnp.int32, (blk, 2 * blk), 0)
        ki = lax.broadcasted_iota(jnp.int32, (blk, 2 * blk), 1)
        dist = qi + blk - ki
        band = (dist >= 0) & (dist < blk)
        distf = dist.astype(_F32)
        for head in range(ATTN_HEADS):
            slope = _LOG2E * 2.0 ** (-8.0 * (head + 1) / ATTN_HEADS)
            bias_s[0, head] = jnp.where(band, -slope * distf, NEG_BIG)
            bias_s[1, head] = jnp.where(band & (ki >= blk), -slope * distf, NEG_BIG)

    @pl.when(first)
    def _():
        k_s[0:blk, :] = jnp.zeros((blk, nkv), _F32)
        v_s[0:blk, :] = jnp.zeros((blk, nkv), _F32)

    def block_norm(b):
        return _rmsnorm(x_ref[b * blk:(b + 1) * blk, :], npre_ref[...]).astype(_MXU_DTYPE)

    def project(hb, b, part):
        rows = slice(b * blk, (b + 1) * blk)
        if part < 2:
            cols = slice(part * (nq // 2), (part + 1) * (nq // 2))
            q_s[rows, cols] = (_dot(hb, w_ref[:, cols]) * (_LOG2E * HEAD_DIM ** -0.5)).astype(q_s.dtype)
        else:
            kv = _dot(hb, w_ref[:, nq:nq + 2 * nkv])
            k_s[blk + b * blk:blk + (b + 1) * blk, :] = kv[:, :nkv]
            v_s[blk + b * blk:blk + (b + 1) * blk, :] = kv[:, nkv:]

    lo = lax.broadcasted_iota(jnp.int32, (1, _LANES), 1) < _HALF

    def block(b, fills):
        r0 = b * blk
        kc = k_s[r0:r0 + 2 * blk, :]
        vc = v_s[r0:r0 + 2 * blk, :]
        kr = pltpu.roll(kc, _HALF, 1)
        vr = pltpu.roll(vc, _HALF, 1)
        k_dup = [jnp.where(lo, kc, kr).astype(_MXU_DTYPE), jnp.where(lo, kr, kc).astype(_MXU_DTYPE)]
        v_half = [
            [jnp.where(lo, vc, 0.0).astype(_MXU_DTYPE), jnp.where(lo, 0.0, vr).astype(_MXU_DTYPE)],
            [jnp.where(lo, vr, 0.0).astype(_MXU_DTYPE), jnp.where(lo, 0.0, vc).astype(_MXU_DTYPE)],
        ]
        no_prev = first.astype(jnp.int32) if b == 0 else 0

        def scores(head):
            pair, half = divmod(head, 2)
            qp = q_s[r0:r0 + blk, pair * _LANES:(pair + 1) * _LANES]
            zero = jnp.zeros_like(qp)
            qm = jnp.where(lo, qp, zero) if half == 0 else jnp.where(lo, zero, qp)
            s = _dot_nt(qm, k_dup[head // group]) + bias_s[no_prev, head]
            sink = sink_ref[head] * _LOG2E
            mx = jnp.maximum(jnp.max(s, axis=-1, keepdims=True), sink)
            pr = jnp.exp2(s - mx)
            denom = jnp.sum(pr, axis=-1, keepdims=True) + jnp.exp2(sink - mx)
            return pr.astype(_MXU_DTYPE), denom

        acc = {}

        def values(head, pr, denom):
            pair, half = divmod(head, 2)
            o = _dot(pr, v_half[head // group][half]) / denom
            if half == 0:
                acc[pair] = o
            else:
                o_ref[r0:r0 + blk, pair * _LANES:(pair + 1) * _LANES] = (acc.pop(pair) + o).astype(o_ref.dtype)

        pending = []
        for head in range(ATTN_HEADS):
            if head in fills:
                fills[head]()
            pending.append((head,) + scores(head))
            if len(pending) > _ATTN_SKEW:
                values(*pending.pop(0))
        while pending:
            values(*pending.pop(0))

    n_blk = tm // blk
    hb = block_norm(0)
    for part in range(3):
        project(hb, 0, part)
    for b in range(n_blk):
        fills = {}
        if b + 1 < n_blk:
            hb = block_norm(b + 1)
            fills = {head: functools.partial(project, hb, b + 1, part)
                     for part, head in enumerate(_ATTN_FILL_HEADS)}
        block(b, fills)
    k_s[0:blk, :] = k_s[tm:tm + blk, :]
    v_s[0:blk, :] = v_s[tm:tm + blk, :]


def _attn_branch(x, norm_pre, w_in_attn, sinks, seq):
    m, d = x.shape
    nq = ATTN_HEADS * HEAD_DIM
    nkv = ATTN_KV_HEADS * HEAD_DIM
    tm = _pick(seq, _ATTN_TM)
    assert tm % ATTN_BLOCK == 0 and nkv == _LANES
    return pl.pallas_call(
        functools.partial(_attn_kernel, seq_tiles=seq // tm),
        grid=(m // tm,),
        in_specs=[
            pl.BlockSpec(memory_space=pltpu.SMEM),
            pl.BlockSpec((tm, d), lambda i: (i, 0)),
            _const_spec((1, d)),
            _const_spec((d, nq + 2 * nkv)),
        ],
        out_specs=pl.BlockSpec((tm, nq), lambda i: (i, 0)),
        out_shape=jax.ShapeDtypeStruct((m, nq), _MXU_DTYPE),
        scratch_shapes=[
            pltpu.VMEM((tm, nq), _MXU_DTYPE),
            pltpu.VMEM((tm + ATTN_BLOCK, nkv), _F32),
            pltpu.VMEM((tm + ATTN_BLOCK, nkv), _F32),
            pltpu.VMEM((2, ATTN_HEADS, ATTN_BLOCK, 2 * ATTN_BLOCK), _F32),
        ],
        compiler_params=_params("arbitrary"),
        name="attn_branch",
    )(sinks, x, norm_pre.reshape(1, d), w_in_attn)


def _hgrn_kernel(x_ref, npre_ref, w_ref, lbl_ref, nw_ref, o_ref, pr_s, st_s, bak_s, *, seq_tiles, layer):
    ch = _HGRN_CHUNK
    tm = x_ref.shape[0]
    nk = HGRN_HEADS * HGRN_DK
    nv = HGRN_HEADS * HGRN_DV
    first = (pl.program_id(0) % seq_tiles) == 0

    @pl.when(first)
    def _():
        st_s[...] = jnp.zeros_like(st_s)

    lg = lbl_ref[...]
    e = jnp.exp(lg - jnp.max(lg, axis=0, keepdims=True))
    sm = e / jnp.sum(e, axis=0, keepdims=True)
    lb = jnp.zeros((1, nk), _F32)
    for prev in range(layer):
        lb = lb + sm[prev:prev + 1, :]
    lb_floor = jnp.maximum(lb, LB_FLOOR)
    one_minus_lb = 1.0 - lb

    lo = lax.broadcasted_iota(jnp.int32, (1, _LANES), 1) < _HALF
    rid = lax.broadcasted_iota(jnp.int32, (ch, nk), 0)
    tril = lax.broadcasted_iota(jnp.int32, (ch, ch), 0) >= lax.broadcasted_iota(jnp.int32, (ch, ch), 1)
    nw = nw_ref[...]

    def chunk(c, a_min, *, exact, fill=(None, None)):
        rows = slice(c * ch, (c + 1) * ch) if isinstance(c, int) else pl.ds(pl.multiple_of(c * ch, ch), ch)
        if fill[0] is not None:
            fill[0]()
        q = pr_s[rows, 0:nk]
        z = pr_s[rows, nk:2 * nk]
        qs = q * jax.nn.sigmoid(q)
        sig_pos = jax.nn.sigmoid(z)
        sig_neg = 1.0 - sig_pos
        log_f = jnp.log(sig_pos + lb_floor * sig_neg)
        kk = one_minus_lb * sig_neg
        a = log_f
        step = 1
        while step < ch:
            a = a + jnp.where(rid >= step, pltpu.roll(a, step, 0), 0.0)
            step *= 2
        a_last = a[ch - 1:ch, :]
        q_dec = (qs * jnp.exp(a)).astype(_MXU_DTYPE)
        k_end = (kk * jnp.exp(a_last - a)).astype(_MXU_DTYPE)
        d_end = jnp.exp(a_last)
        heads = range(HGRN_HEADS)
        sls = [slice(hd * HGRN_DK, (hd + 1) * HGRN_DK) for hd in heads]
        vps = [pr_s[rows, 2 * nk + pr * _LANES:2 * nk + (pr + 1) * _LANES] for pr in range(HGRN_HEADS // 2)]
        vpm = [v.astype(_MXU_DTYPE) for v in vps]
        if exact:
            sid_v = lax.broadcasted_iota(jnp.int32, (ch, _LANES), 0)

            def query_row(t, intra):
                a_t = jnp.sum(jnp.where(rid == t, a, 0.0), axis=0, keepdims=True)
                q_t = jnp.sum(jnp.where(rid == t, qs, 0.0), axis=0, keepdims=True)
                wts = q_t * jnp.exp(jnp.where(rid <= t, a_t - a, NEG_BIG)) * kk
                out = []
                for hd in heads:
                    col = jnp.sum(wts[:, sls[hd]], axis=-1, keepdims=True)
                    o_t = jnp.sum(col * vps[hd // 2], axis=0, keepdims=True)
                    out.append(jnp.where(sid_v == t, o_t, intra[hd]))
                return tuple(out)

            intra = lax.fori_loop(0, ch, query_row, tuple(jnp.zeros((ch, _LANES), _F32) for _ in heads))
        else:
            k_inv = (kk * jnp.exp(-a)).astype(_MXU_DTYPE)
            att = [_dot_nt(q_dec[:, sls[hd]], k_inv[:, sls[hd]]) for hd in heads]
        upd_pair = [_dot_tn(vpm[pr], k_end[:, 2 * pr * HGRN_DK:2 * (pr + 1) * HGRN_DK])
                    for pr in range(HGRN_HEADS // 2)]
        upd = [upd_pair[hd // 2][:, (hd % 2) * HGRN_DK:(hd % 2 + 1) * HGRN_DK] for hd in heads]
        states = [st_s[hd] for hd in heads]
        inter = [_dot_nt(q_dec[:, sls[hd]], states[hd].astype(_MXU_DTYPE)) for hd in heads]
        if fill[1] is not None:
            fill[1]()
        if not exact:
            att_m = [jnp.where(tril, att[hd], 0.0).astype(_MXU_DTYPE) for hd in heads]
            intra_pair = [_dot(jnp.concatenate(att_m[2 * pr:2 * pr + 2], axis=0), vpm[pr])
                          for pr in range(HGRN_HEADS // 2)]
            intra = [intra_pair[hd // 2][(hd % 2) * ch:(hd % 2 + 1) * ch, :] for hd in heads]
        for hd in heads:
            st_s[hd] = states[hd] * d_end[:, sls[hd]] + upd[hd]
        for pair in range(HGRN_HEADS // 2):
            gp = pr_s[rows, 2 * nk + nv + pair * _LANES:2 * nk + nv + (pair + 1) * _LANES]
            outs = [inter[2 * pair + half] + intra[2 * pair + half] for half in range(2)]
            o = jnp.where(lo, outs[0], outs[1])
            osq = o * o
            ms_lo = jnp.sum(jnp.where(lo, osq, 0.0), axis=-1, keepdims=True) * (1.0 / HGRN_DV)
            ms_hi = jnp.sum(jnp.where(lo, 0.0, osq), axis=-1, keepdims=True) * (1.0 / HGRN_DV)
            rinv = jnp.where(lo, lax.rsqrt(ms_lo + NORM_EPS), lax.rsqrt(ms_hi + NORM_EPS))
            on = o * rinv * nw[:, pair * _LANES:(pair + 1) * _LANES]
            o_ref[rows, pair * _LANES:(pair + 1) * _LANES] = (on * (gp * jax.nn.sigmoid(gp))).astype(o_ref.dtype)
        return jnp.minimum(a_min, a_last)

    bak_s[...] = st_s[...]

    rg = 2 * ch
    n_rg = tm // rg
    n_sl = 4
    scols = (2 * nk + 2 * nv) // n_sl

    def group_norm(g):
        return _rmsnorm(x_ref[g * rg:(g + 1) * rg, :], npre_ref[...]).astype(_MXU_DTYPE)

    def project(hg, g, s):
        cols = slice(s * scols, (s + 1) * scols)
        pr_s[g * rg:(g + 1) * rg, cols] = _dot(hg, w_ref[:, cols])

    hg = group_norm(0)
    for s in range(n_sl):
        project(hg, 0, s)
    a_min = jnp.zeros((1, nk), _F32)
    for c in range(tm // ch):
        g, part = divmod(c, 2)
        fill = (None, None)
        if g + 1 < n_rg:
            if part == 0:
                hg = group_norm(g + 1)
            fill = tuple(functools.partial(project, hg, g + 1, 2 * part + k) for k in range(2))
        a_min = chunk(c, a_min, exact=False, fill=fill)

    @pl.when(jnp.min(a_min) < -_HGRN_SAFE_LOG_DECAY)
    def _():
        st_s[...] = bak_s[...]
        lax.fori_loop(0, tm // ch, functools.partial(chunk, exact=True), jnp.zeros((1, nk), _F32))


def _hgrn_branch(x, norm_pre, w_in_hgrn, lb_logits, norm_w, seq, layer):
    m, d = x.shape
    nk = HGRN_HEADS * HGRN_DK
    nv = HGRN_HEADS * HGRN_DV
    depth = lb_logits.shape[0]
    tm = _pick(seq, _HGRN_TM)
    assert HGRN_DK == _LANES and 2 * HGRN_DV == _LANES and tm % _HGRN_CHUNK == 0
    nw = jnp.tile(norm_w.reshape(1, HGRN_DV), (1, HGRN_HEADS))
    return pl.pallas_call(
        functools.partial(_hgrn_kernel, seq_tiles=seq // tm, layer=layer),
        grid=(m // tm,),
        in_specs=[
            pl.BlockSpec((tm, d), lambda i: (i, 0)),
            _const_spec((1, d)),
            _const_spec((d, 2 * nk + 2 * nv)),
            _const_spec((depth, nk)),
            _const_spec((1, nv)),
        ],
        out_specs=pl.BlockSpec((tm, nv), lambda i: (i, 0)),
        out_shape=jax.ShapeDtypeStruct((m, nv), _MXU_DTYPE),
        scratch_shapes=[
            pltpu.VMEM((tm, 2 * nk + 2 * nv), _F32),
            pltpu.VMEM((HGRN_HEADS, _LANES, HGRN_DK), _F32),
            pltpu.VMEM((HGRN_HEADS, _LANES, HGRN_DK), _F32),
        ],
        compiler_params=_params("arbitrary"),
        name="hgrn_branch",
    )(x, norm_pre.reshape(1, d), w_in_hgrn, lb_logits, nw)


def _merge_kernel(x_ref, npre_ref, wgc_ref, wga_ref, wgh_ref, yc_ref, ya_ref, yh_ref,
                  wbc_ref, wba_ref, wbh_ref, wo_ref, npost_ref, o_ref, h_ref, **steps):
    def partial(h, rows):
        merged = (jax.nn.sigmoid(_dot(h, wgc_ref[...])) * _dot(yc_ref[rows, :], wbc_ref[...])
                  + jax.nn.sigmoid(_dot(h, wga_ref[...])) * _dot(ya_ref[rows, :], wba_ref[...])
                  + jax.nn.sigmoid(_dot(h, wgh_ref[...])) * _dot(yh_ref[rows, :], wbh_ref[...]))
        return _dot(merged.astype(_MXU_DTYPE), wo_ref[...])

    _reduce_steps(x_ref, npre_ref, npost_ref, o_ref, h_ref, partial, scale=1.0, **steps)


def _merge_step(x, norm_pre, w_gates, y_conv, y_attn, y_hgrn, wb_conv, wb_attn, wb_hgrn, w_o, norm_post):
    m, d = x.shape
    tm, tn = _pick(m, _MIX_TM), _pick(d, _MIX_TN)
    nt = d // tn
    nc, na, nh = y_conv.shape[1], y_attn.shape[1], y_hgrn.shape[1]
    return pl.pallas_call(
        functools.partial(_merge_kernel, n_steps=nt, rb=min(tm, _FFN_RB), rb_edge=min(tm, _ROW_BLOCK)),
        grid=(m // tm, nt),
        in_specs=[
            pl.BlockSpec((tm, d), lambda i, j: (i, 0)),
            pl.BlockSpec((1, d), lambda i, j: (0, 0)),
            pl.BlockSpec((d, tn), lambda i, j: (0, j)),
            pl.BlockSpec((d, tn), lambda i, j: (0, j + nt)),
            pl.BlockSpec((d, tn), lambda i, j: (0, j + 2 * nt)),
            pl.BlockSpec((tm, nc), lambda i, j: (i, 0)),
            pl.BlockSpec((tm, na), lambda i, j: (i, 0)),
            pl.BlockSpec((tm, nh), lambda i, j: (i, 0)),
            pl.BlockSpec((nc, tn), lambda i, j: (0, j)),
            pl.BlockSpec((na, tn), lambda i, j: (0, j)),
            pl.BlockSpec((nh, tn), lambda i, j: (0, j)),
            pl.BlockSpec((tn, d), lambda i, j: (j, 0)),
            pl.BlockSpec((1, d), lambda i, j: (0, 0)),
        ],
        out_specs=pl.BlockSpec((tm, d), lambda i, j: (i, 0)),
        out_shape=jax.ShapeDtypeStruct((m, d), _F32),
        scratch_shapes=[pltpu.VMEM((tm, d), _MXU_DTYPE)],
        compiler_params=_params("parallel", "arbitrary"),
        name="merge_step",
    )(x, norm_pre.reshape(1, d), w_gates, w_gates, w_gates, y_conv, y_attn, y_hgrn,
      wb_conv, wb_attn, wb_hgrn, w_o, norm_post.reshape(1, d))


def kernel(x, p, ffn1_norm_pre, ffn1_w_gu, ffn1_w_down, ffn1_norm_post, mix_norm_pre, w_in, conv_w, attn_sinks, hgrn_lb_logits, hgrn_norm, w_branch_conv, w_branch_attn, w_branch_hgrn, w_o, mix_norm_post, ffn2_norm_pre, ffn2_w_gu, ffn2_w_down, ffn2_norm_post, ple_norm_pre, w_ple_gate, w_ple_proj, ple_norm_post):
    batch, seq, d = x.shape
    depth = p.shape[0]
    m = batch * seq
    n_conv = 3 * CONV_WIDTH
    n_attn = (ATTN_HEADS + 2 * ATTN_KV_HEADS) * HEAD_DIM
    n_hgrn = 2 * HGRN_HEADS * (HGRN_DK + HGRN_DV)

    xf = x.reshape(m, d)
    for l in range(depth):
        xf = _ffn_half_step(xf, ffn1_norm_pre[l], _to_mxu(ffn1_w_gu, l), _to_mxu(ffn1_w_down, l), ffn1_norm_post[l])
        w_conv, w_attn, w_hgrn, w_gates = _split_to_mxu(w_in, l, (n_conv, n_attn, n_hgrn, 3 * d))
        y_conv = _conv_branch(xf, mix_norm_pre[l], w_conv, conv_w[l], seq)
        y_attn = _attn_branch(xf, mix_norm_pre[l], w_attn, attn_sinks[l], seq)
        y_hgrn = _hgrn_branch(xf, mix_norm_pre[l], w_hgrn, hgrn_lb_logits, hgrn_norm[l], seq, l)
        xf = _merge_step(xf, mix_norm_pre[l], w_gates, y_conv, y_attn, y_hgrn,
                         _to_mxu(w_branch_conv, l), _to_mxu(w_branch_attn, l), _to_mxu(w_branch_hgrn, l),
                         _to_mxu(w_o, l), mix_norm_post[l])
        xf = _ffn_half_step(xf, ffn2_norm_pre[l], _to_mxu(ffn2_w_gu, l), _to_mxu(ffn2_w_down, l), ffn2_norm_post[l])
        xf = _ple_step(xf, p.reshape(depth, m, -1), l, ple_norm_pre[l], _to_mxu(w_ple_gate, l),
                       _to_mxu(w_ple_proj, l), ple_norm_post[l])
    return xf.reshape(batch, seq, d)
```

```python
import functools
import math

import jax
import jax.numpy as jnp
from jax import lax
from jax.experimental import pallas as pl
from jax.experimental.pallas import tpu as pltpu

_MXU_DTYPE = jnp.bfloat16
_F32 = jnp.float32

NORM_EPS = 1e-6
NEG_BIG = -1e30
LB_FLOOR = 1e-20

CONV_WIDTH = 512
CONV_K = 3
ATTN_HEADS = 16
ATTN_KV_HEADS = 2
HEAD_DIM = 64
ATTN_BLOCK = 128
HGRN_HEADS = 8
HGRN_DK = 128
HGRN_DV = 64

_LANES = 128
_HALF = 64
_LOG2E = math.log2(math.e)

_HGRN_CHUNK = 64
_HGRN_SAFE_LOG_DECAY = 80.0

_FFN_TM = 1024
_FFN_RB = 512
_FFN_TF = 512
_MIX_TM = 512
_MIX_TN = 512
_PLE_TM = 1024
_CONV_TM = 512
_ROW_BLOCK = 256
_CAST_ROWS = 512
_CAST_COLS = 2816
_SPLIT_ROWS = 128
_ATTN_TM = 512
_HGRN_TM = 512

_VMEM_LIMIT = 60 * 1024 * 1024


def _pick(n, pref):
    t = min(n, pref)
    assert n % t == 0, (n, pref)
    return t


def _rmsnorm(xf, w):
    return xf * lax.rsqrt(jnp.mean(xf * xf, axis=-1, keepdims=True) + NORM_EPS) * w


def _reduce_steps(x_ref, npre_ref, npost_ref, o_ref, h_ref, partial, *, n_steps, rb, rb_edge, scale):
    j = pl.program_id(1)
    tm = x_ref.shape[0]

    def run(first, last, rows_per_block):
        w_post = npost_ref[...] * scale
        for r in range(tm // rows_per_block):
            rows = slice(r * rows_per_block, (r + 1) * rows_per_block)
            if first:
                h = _rmsnorm(x_ref[rows, :], npre_ref[...]).astype(h_ref.dtype)
                h_ref[rows, :] = h
            else:
                h = h_ref[rows, :]
            acc = partial(h, rows)
            if not first:
                acc = o_ref[rows, :] + acc
            o_ref[rows, :] = x_ref[rows, :] + _rmsnorm(acc, w_post) if last else acc

    if n_steps == 1:
        run(True, True, rb_edge)
        return
    pl.when(j == 0)(functools.partial(run, True, False, rb_edge))
    if n_steps > 2:
        pl.when(jnp.logical_and(j > 0, j < n_steps - 1))(functools.partial(run, False, False, rb))
    pl.when(j == n_steps - 1)(functools.partial(run, False, True, rb_edge))


def _dot(a, b):
    return jnp.dot(a, b, preferred_element_type=_F32)


def _dot_nt(a, b):
    return lax.dot_general(a, b, (((1,), (1,)), ((), ())), preferred_element_type=_F32)


def _dot_tn(a, b):
    return lax.dot_general(a, b, (((0,), (0,)), ((), ())), preferred_element_type=_F32)


def _params(*sem):
    return pltpu.CompilerParams(dimension_semantics=sem, vmem_limit_bytes=_VMEM_LIMIT)


def _const_spec(shape):
    nd = len(shape)
    return pl.BlockSpec(shape, lambda *_: (0,) * nd, pipeline_mode=pl.Buffered(1))


def _cast_kernel(w_ref, o_ref):
    o_ref[...] = w_ref[...].astype(o_ref.dtype)


def _to_mxu(w, layer):
    _, r, c = w.shape
    tc = max(t for t in range(_LANES, min(c, _CAST_COLS) + 1, _LANES) if c % t == 0)
    tr = _pick(r, _CAST_ROWS)
    return pl.pallas_call(
        _cast_kernel,
        grid=(r // tr, c // tc),
        in_specs=[pl.BlockSpec((None, tr, tc), lambda i, j: (layer, i, j))],
        out_specs=pl.BlockSpec((tr, tc), lambda i, j: (i, j)),
        out_shape=jax.ShapeDtypeStruct((r, c), _MXU_DTYPE),
        compiler_params=_params("parallel", "parallel"),
        name="weight_cast",
    )(w)


def _split_cast_kernel(w_ref, *o_refs):
    col = 0
    for o_ref in o_refs:
        o_ref[...] = w_ref[:, col:col + o_ref.shape[1]].astype(o_ref.dtype)
        col += o_ref.shape[1]


def _split_to_mxu(w, layer, widths):
    _, r, c = w.shape
    assert sum(widths) == c and all(n % _LANES == 0 for n in widths)
    tr = _pick(r, _SPLIT_ROWS)
    return pl.pallas_call(
        _split_cast_kernel,
        grid=(r // tr,),
        in_specs=[pl.BlockSpec((None, tr, c), lambda i: (layer, i, 0))],
        out_specs=[pl.BlockSpec((tr, n), lambda i: (i, 0)) for n in widths],
        out_shape=[jax.ShapeDtypeStruct((r, n), _MXU_DTYPE) for n in widths],
        compiler_params=_params("parallel"),
        name="weight_split_cast",
    )(w)


def _ffn_kernel(x_ref, npre_ref, wg_ref, wu_ref, wd_ref, npost_ref, o_ref, h_ref, **steps):
    def partial(h, rows):
        g = _dot(h, wg_ref[...])
        u = _dot(h, wu_ref[...])
        a = (g * jax.nn.sigmoid(g) * u).astype(_MXU_DTYPE)
        return _dot(a, wd_ref[...])

    _reduce_steps(x_ref, npre_ref, npost_ref, o_ref, h_ref, partial, scale=0.5, **steps)


def _ffn_half_step(x, norm_pre, w_gu, w_down, norm_post):
    m, d = x.shape
    f = w_down.shape[0]
    tm, tf = _pick(m, _FFN_TM), _pick(f, _FFN_TF)
    nf = f // tf
    return pl.pallas_call(
        functools.partial(_ffn_kernel, n_steps=nf, rb=min(tm, _FFN_RB), rb_edge=min(tm, _FFN_RB)),
        grid=(m // tm, nf),
        in_specs=[
            pl.BlockSpec((tm, d), lambda i, j: (i, 0)),
            pl.BlockSpec((1, d), lambda i, j: (0, 0)),
            pl.BlockSpec((d, tf), lambda i, j: (0, j)),
            pl.BlockSpec((d, tf), lambda i, j: (0, j + nf)),
            pl.BlockSpec((tf, d), lambda i, j: (j, 0)),
            pl.BlockSpec((1, d), lambda i, j: (0, 0)),
        ],
        out_specs=pl.BlockSpec((tm, d), lambda i, j: (i, 0)),
        out_shape=jax.ShapeDtypeStruct((m, d), _F32),
        scratch_shapes=[pltpu.VMEM((tm, d), _MXU_DTYPE)],
        compiler_params=_params("parallel", "arbitrary"),
        name="ffn_half_step",
    )(x, norm_pre.reshape(1, d), w_gu, w_gu, w_down, norm_post.reshape(1, d))


def _ple_kernel(x_ref, p_ref, npre_ref, wg_ref, wp_ref, npost_ref, o_ref, *, rb):
    for r in range(x_ref.shape[0] // rb):
        rows = slice(r * rb, (r + 1) * rb)
        x = x_ref[rows, :]
        hp = _rmsnorm(x, npre_ref[...]).astype(_MXU_DTYPE)
        gate = jax.nn.sigmoid(_dot(hp, wg_ref[...]))
        proj = _dot(p_ref[rows, :].astype(_MXU_DTYPE), wp_ref[...])
        o_ref[rows, :] = x + _rmsnorm(gate * proj, npost_ref[...])


def _ple_step(x, p, layer, norm_pre, w_gate, w_proj, norm_post):
    m, d = x.shape
    dp = p.shape[2]
    tm = _pick(m, _PLE_TM)
    return pl.pallas_call(
        functools.partial(_ple_kernel, rb=min(tm, _ROW_BLOCK)),
        grid=(m // tm,),
        in_specs=[
            pl.BlockSpec((tm, d), lambda i: (i, 0)),
            pl.BlockSpec((None, tm, dp), lambda i: (layer, i, 0)),
            _const_spec((1, d)),
            _const_spec((d, d)),
            _const_spec((dp, d)),
            _const_spec((1, d)),
        ],
        out_specs=pl.BlockSpec((tm, d), lambda i: (i, 0)),
        out_shape=jax.ShapeDtypeStruct((m, d), _F32),
        compiler_params=_params("parallel"),
        name="ple_step",
    )(x, p, norm_pre.reshape(1, d), w_gate, w_proj, norm_post.reshape(1, d))


def _conv_kernel(x_ref, npre_ref, w_ref, cw_ref, o_ref, carry_ref, *, seq_tiles, rb):
    first = (pl.program_id(0) % seq_tiles) == 0
    tm = x_ref.shape[0]
    cwid = o_ref.shape[1]

    @pl.when(first)
    def _():
        carry_ref[...] = jnp.zeros_like(carry_ref)

    cw = cw_ref[...]
    row = lax.broadcasted_iota(jnp.int32, (rb, cwid), 0)
    tail = carry_ref[...]
    for r in range(tm // rb):
        rows = slice(r * rb, (r + 1) * rb)
        h = _rmsnorm(x_ref[rows, :], npre_ref[...]).astype(_MXU_DTYPE)
        proj = _dot(h, w_ref[...])
        xin, b_gate, c_gate = proj[:, :cwid], proj[:, cwid:2 * cwid], proj[:, 2 * cwid:]
        u = c_gate * xin
        prev1 = tail[7:8, :]
        prev2 = tail[6:7, :]
        u1 = jnp.where(row == 0, prev1, pltpu.roll(u, 1, 0))
        u2 = jnp.where(row == 0, prev2, jnp.where(row == 1, prev1, pltpu.roll(u, 2, 0)))
        y = cw[2:3, :] * u + cw[1:2, :] * u1 + cw[0:1, :] * u2
        o_ref[rows, :] = (b_gate * y).astype(o_ref.dtype)
        tail = u[rb - 8:, :]
    carry_ref[...] = tail


def _conv_branch(x, norm_pre, w_in_conv, conv_w, seq):
    m, d = x.shape
    cwid = conv_w.shape[1]
    tm = _pick(seq, _CONV_TM)
    cw = jnp.zeros((8, cwid), _F32).at[:CONV_K].set(conv_w)
    return pl.pallas_call(
        functools.partial(_conv_kernel, seq_tiles=seq // tm, rb=min(tm, _ROW_BLOCK)),
        grid=(m // tm,),
        in_specs=[
            pl.BlockSpec((tm, d), lambda i: (i, 0)),
            _const_spec((1, d)),
            _const_spec((d, 3 * cwid)),
            _const_spec((8, cwid)),
        ],
        out_specs=pl.BlockSpec((tm, cwid), lambda i: (i, 0)),
        out_shape=jax.ShapeDtypeStruct((m, cwid), _MXU_DTYPE),
        scratch_shapes=[pltpu.VMEM((8, cwid), _F32)],
        compiler_params=_params("arbitrary"),
        name="conv_branch",
    )(x, norm_pre.reshape(1, d), w_in_conv, cw)


def _attn_kernel(sink_ref, x_ref, npre_ref, w_ref, o_ref, q_s, k_s, v_s, bias_s, *, seq_tiles):
    blk = ATTN_BLOCK
    tm = x_ref.shape[0]
    nq = ATTN_HEADS * HEAD_DIM
    nkv = ATTN_KV_HEADS * HEAD_DIM
    first = (pl.program_id(0) % seq_tiles) == 0
    group = ATTN_HEADS // ATTN_KV_HEADS

    @pl.when(pl.program_id(0) == 0)
    def _():
        qi = lax.broadcasted_iota(jnp.int32, (blk, 2 * blk), 0)
        ki = lax.broadcasted_iota(jnp.int32, (blk, 2 * blk), 1)
        dist = qi + blk - ki
        band = (dist >= 0) & (dist < blk)
        distf = dist.astype(_F32)
        for head in range(ATTN_HEADS):
            slope = _LOG2E * 2.0 ** (-8.0 * (head + 1) / ATTN_HEADS)
            bias_s[0, head] = jnp.where(band, -slope * distf, NEG_BIG)
            bias_s[1, head] = jnp.where(band & (ki >= blk), -slope * distf, NEG_BIG)

    @pl.when(first)
    def _():
        k_s[0:blk, :] = jnp.zeros((blk, nkv), _F32)
        v_s[0:blk, :] = jnp.zeros((blk, nkv), _F32)

    def block_norm(b):
        return _rmsnorm(x_ref[b * blk:(b + 1) * blk, :], npre_ref[...]).astype(_MXU_DTYPE)

    def project(hb, b, part):
        rows = slice(b * blk, (b + 1) * blk)
        if part < 2:
            cols = slice(part * (nq // 2), (part + 1) * (nq // 2))
            q_s[rows, cols] = (_dot(hb, w_ref[:, cols]) * (_LOG2E * HEAD_DIM ** -0.5)).astype(q_s.dtype)
        else:
            kv = _dot(hb, w_ref[:, nq:nq + 2 * nkv])
            k_s[blk + b * blk:blk + (b + 1) * blk, :] = kv[:, :nkv]
            v_s[blk + b * blk:blk + (b + 1) * blk, :] = kv[:, nkv:]

    lo = lax.broadcasted_iota(jnp.int32, (1, _LANES), 1) < _HALF

    def block(b, fills):
        r0 = b * blk
        kc = k_s[r0:r0 + 2 * blk, :]
        vc = v_s[r0:r0 + 2 * blk, :]
        kr = pltpu.roll(kc, _HALF, 1)
        vr = pltpu.roll(vc, _HALF, 1)
        k_dup = [jnp.where(lo, kc, kr).astype(_MXU_DTYPE), jnp.where(lo, kr, kc).astype(_MXU_DTYPE)]
        v_half = [
            [jnp.where(lo, vc, 0.0).astype(_MXU_DTYPE), jnp.where(lo, 0.0, vr).astype(_MXU_DTYPE)],
            [jnp.where(lo, vr, 0.0).astype(_MXU_DTYPE), jnp.where(lo, 0.0, vc).astype(_MXU_DTYPE)],
        ]
        no_prev = first.astype(jnp.int32) if b == 0 else 0
        q_rows = q_s[r0:r0 + blk, :]

        def group_scores(kv):
            qm = []
            for head in range(kv * group, (kv + 1) * group):
                qp = q_rows[:, (head // 2) * _LANES:(head // 2 + 1) * _LANES]
                zero = jnp.zeros_like(qp)
                qm.append(jnp.where(lo, qp, zero) if head % 2 == 0 else jnp.where(lo, zero, qp))
            return _dot_nt(jnp.concatenate(qm, axis=0), k_dup[kv])

        def group_softmax(kv, s_all):
            probs, denoms = [], []
            for i, head in enumerate(range(kv * group, (kv + 1) * group)):
                s = s_all[i * blk:(i + 1) * blk, :] + bias_s[no_prev, head]
                sink = sink_ref[head] * _LOG2E
                mx = jnp.maximum(jnp.max(s, axis=-1, keepdims=True), sink)
                pr = jnp.exp2(s - mx)
                denoms.append(jnp.sum(pr, axis=-1, keepdims=True) + jnp.exp2(sink - mx))
                probs.append(pr.astype(_MXU_DTYPE))
            return probs, denoms

        def group_values(kv, probs, denoms):
            halves = [_dot(jnp.concatenate(probs[half::2], axis=0), v_half[kv][half]) for half in range(2)]
            for j in range(group // 2):
                rows = slice(j * blk, (j + 1) * blk)
                pair = kv * (group // 2) + j
                o = halves[0][rows, :] / denoms[2 * j] + halves[1][rows, :] / denoms[2 * j + 1]
                o_ref[r0:r0 + blk, pair * _LANES:(pair + 1) * _LANES] = o.astype(o_ref.dtype)

        fills[0]()
        s0 = group_scores(0)
        s1 = group_scores(1)
        fills[1]()
        sm0 = group_softmax(0, s0)
        group_values(0, *sm0)
        fills[2]()
        sm1 = group_softmax(1, s1)
        group_values(1, *sm1)

    n_blk = tm // blk
    hb = block_norm(0)
    for part in range(3):
        project(hb, 0, part)
    for b in range(n_blk):
        fills = [lambda: None] * 3
        if b + 1 < n_blk:
            hb = block_norm(b + 1)
            fills = [functools.partial(project, hb, b + 1, part) for part in range(3)]
        block(b, fills)
    k_s[0:blk, :] = k_s[tm:tm + blk, :]
    v_s[0:blk, :] = v_s[tm:tm + blk, :]


def _attn_branch(x, norm_pre, w_in_attn, sinks, seq):
    m, d = x.shape
    nq = ATTN_HEADS * HEAD_DIM
    nkv = ATTN_KV_HEADS * HEAD_DIM
    tm = _pick(seq, _ATTN_TM)
    assert tm % ATTN_BLOCK == 0 and nkv == _LANES
    return pl.pallas_call(
        functools.partial(_attn_kernel, seq_tiles=seq // tm),
        grid=(m // tm,),
        in_specs=[
            pl.BlockSpec(memory_space=pltpu.SMEM),
            pl.BlockSpec((tm, d), lambda i: (i, 0)),
            _const_spec((1, d)),
            _const_spec((d, nq + 2 * nkv)),
        ],
        out_specs=pl.BlockSpec((tm, nq), lambda i: (i, 0)),
        out_shape=jax.ShapeDtypeStruct((m, nq), _MXU_DTYPE),
        scratch_shapes=[
            pltpu.VMEM((tm, nq), _MXU_DTYPE),
            pltpu.VMEM((tm + ATTN_BLOCK, nkv), _F32),
            pltpu.VMEM((tm + ATTN_BLOCK, nkv), _F32),
            pltpu.VMEM((2, ATTN_HEADS, ATTN_BLOCK, 2 * ATTN_BLOCK), _F32),
        ],
        compiler_params=_params("arbitrary"),
        name="attn_branch",
    )(sinks, x, norm_pre.reshape(1, d), w_in_attn)


def _hgrn_kernel(x_ref, npre_ref, w_ref, lbl_ref, nw_ref, o_ref, pr_s, st_s, bak_s, *, seq_tiles, layer):
    ch = _HGRN_CHUNK
    tm = x_ref.shape[0]
    nk = HGRN_HEADS * HGRN_DK
    nv = HGRN_HEADS * HGRN_DV
    first = (pl.program_id(0) % seq_tiles) == 0

    @pl.when(first)
    def _():
        st_s[...] = jnp.zeros_like(st_s)

    lg = lbl_ref[...]
    e = jnp.exp(lg - jnp.max(lg, axis=0, keepdims=True))
    sm = e / jnp.sum(e, axis=0, keepdims=True)
    lb = jnp.zeros((1, nk), _F32)
    for prev in range(layer):
        lb = lb + sm[prev:prev + 1, :]
    lb_floor = jnp.maximum(lb, LB_FLOOR)
    one_minus_lb = 1.0 - lb

    lo = lax.broadcasted_iota(jnp.int32, (1, _LANES), 1) < _HALF
    rid = lax.broadcasted_iota(jnp.int32, (ch, nk), 0)
    tril = lax.broadcasted_iota(jnp.int32, (ch, ch), 0) >= lax.broadcasted_iota(jnp.int32, (ch, ch), 1)
    nw = nw_ref[...]

    def chunk(c, a_min, *, exact, fill=(None, None)):
        rows = slice(c * ch, (c + 1) * ch) if isinstance(c, int) else pl.ds(pl.multiple_of(c * ch, ch), ch)
        if fill[0] is not None:
            fill[0]()
        q = pr_s[rows, 0:nk]
        z = pr_s[rows, nk:2 * nk]
        qs = q * jax.nn.sigmoid(q)
        sig_pos = jax.nn.sigmoid(z)
        sig_neg = 1.0 - sig_pos
        log_f = jnp.log(sig_pos + lb_floor * sig_neg)
        kk = one_minus_lb * sig_neg
        a = log_f
        step = 1
        while step < ch:
            a = a + jnp.where(rid >= step, pltpu.roll(a, step, 0), 0.0)
            step *= 2
        a_last = a[ch - 1:ch, :]
        q_dec = (qs * jnp.exp(a)).astype(_MXU_DTYPE)
        k_end = (kk * jnp.exp(a_last - a)).astype(_MXU_DTYPE)
        d_end = jnp.exp(a_last)
        heads = range(HGRN_HEADS)
        sls = [slice(hd * HGRN_DK, (hd + 1) * HGRN_DK) for hd in heads]
        vps = [pr_s[rows, 2 * nk + pr * _LANES:2 * nk + (pr + 1) * _LANES] for pr in range(HGRN_HEADS // 2)]
        vpm = [v.astype(_MXU_DTYPE) for v in vps]
        if exact:
            sid_v = lax.broadcasted_iota(jnp.int32, (ch, _LANES), 0)

            def query_row(t, intra):
                a_t = jnp.sum(jnp.where(rid == t, a, 0.0), axis=0, keepdims=True)
                q_t = jnp.sum(jnp.where(rid == t, qs, 0.0), axis=0, keepdims=True)
                wts = q_t * jnp.exp(jnp.where(rid <= t, a_t - a, NEG_BIG)) * kk
                out = []
                for hd in heads:
                    col = jnp.sum(wts[:, sls[hd]], axis=-1, keepdims=True)
                    o_t = jnp.sum(col * vps[hd // 2], axis=0, keepdims=True)
                    out.append(jnp.where(sid_v == t, o_t, intra[hd]))
                return tuple(out)

            intra = lax.fori_loop(0, ch, query_row, tuple(jnp.zeros((ch, _LANES), _F32) for _ in heads))
        else:
            k_inv = (kk * jnp.exp(-a)).astype(_MXU_DTYPE)
            att = [_dot_nt(q_dec[:, sls[hd]], k_inv[:, sls[hd]]) for hd in heads]
        upd_pair = [_dot_tn(vpm[pr], k_end[:, 2 * pr * HGRN_DK:2 * (pr + 1) * HGRN_DK])
                    for pr in range(HGRN_HEADS // 2)]
        upd = [upd_pair[hd // 2][:, (hd % 2) * HGRN_DK:(hd % 2 + 1) * HGRN_DK] for hd in heads]
        states = [st_s[hd] for hd in heads]
        inter = [_dot_nt(q_dec[:, sls[hd]], states[hd].astype(_MXU_DTYPE)) for hd in heads]
        if fill[1] is not None:
            fill[1]()
        if not exact:
            att_m = [jnp.where(tril, att[hd], 0.0).astype(_MXU_DTYPE) for hd in heads]
            intra_pair = [_dot(jnp.concatenate(att_m[2 * pr:2 * pr + 2], axis=0), vpm[pr])
                          for pr in range(HGRN_HEADS // 2)]
            intra = [intra_pair[hd // 2][(hd % 2) * ch:(hd % 2 + 1) * ch, :] for hd in heads]
        for hd in heads:
            st_s[hd] = states[hd] * d_end[:, sls[hd]] + upd[hd]
        for pair in range(HGRN_HEADS // 2):
            gp = pr_s[rows, 2 * nk + nv + pair * _LANES:2 * nk + nv + (pair + 1) * _LANES]
            outs = [inter[2 * pair + half] + intra[2 * pair + half] for half in range(2)]
            o = jnp.where(lo, outs[0], outs[1])
            osq = o * o
            ms_lo = jnp.sum(jnp.where(lo, osq, 0.0), axis=-1, keepdims=True) * (1.0 / HGRN_DV)
            ms_hi = jnp.sum(jnp.where(lo, 0.0, osq), axis=-1, keepdims=True) * (1.0 / HGRN_DV)
            rinv = jnp.where(lo, lax.rsqrt(ms_lo + NORM_EPS), lax.rsqrt(ms_hi + NORM_EPS))
            on = o * rinv * nw[:, pair * _LANES:(pair + 1) * _LANES]
            o_ref[rows, pair * _LANES:(pair + 1) * _LANES] = (on * (gp * jax.nn.sigmoid(gp))).astype(o_ref.dtype)
        return jnp.minimum(a_min, a_last)

    bak_s[...] = st_s[...]

    rg = 2 * ch
    n_rg = tm // rg
    n_sl = 4
    scols = (2 * nk + 2 * nv) // n_sl

    def group_norm(g):
        return _rmsnorm(x_ref[g * rg:(g + 1) * rg, :], npre_ref[...]).astype(_MXU_DTYPE)

    def project(hg, g, s):
        cols = slice(s * scols, (s + 1) * scols)
        pr_s[g * rg:(g + 1) * rg, cols] = _dot(hg, w_ref[:, cols])

    hg = group_norm(0)
    for s in range(n_sl):
        project(hg, 0, s)
    a_min = jnp.zeros((1, nk), _F32)
    for c in range(tm // ch):
        g, part = divmod(c, 2)
        fill = (None, None)
        if g + 1 < n_rg:
            if part == 0:
                hg = group_norm(g + 1)
            fill = tuple(functools.partial(project, hg, g + 1, 2 * part + k) for k in range(2))
        a_min = chunk(c, a_min, exact=False, fill=fill)

    @pl.when(jnp.min(a_min) < -_HGRN_SAFE_LOG_DECAY)
    def _():
        st_s[...] = bak_s[...]
        lax.fori_loop(0, tm // ch, functools.partial(chunk, exact=True), jnp.zeros((1, nk), _F32))


def _hgrn_branch(x, norm_pre, w_in_hgrn, lb_logits, norm_w, seq, layer):
    m, d = x.shape
    nk = HGRN_HEADS * HGRN_DK
    nv = HGRN_HEADS * HGRN_DV
    depth = lb_logits.shape[0]
    tm = _pick(seq, _HGRN_TM)
    assert HGRN_DK == _LANES and 2 * HGRN_DV == _LANES and tm % _HGRN_CHUNK == 0
    nw = jnp.tile(norm_w.reshape(1, HGRN_DV), (1, HGRN_HEADS))
    return pl.pallas_call(
        functools.partial(_hgrn_kernel, seq_tiles=seq // tm, layer=layer),
        grid=(m // tm,),
        in_specs=[
            pl.BlockSpec((tm, d), lambda i: (i, 0)),
            _const_spec((1, d)),
            _const_spec((d, 2 * nk + 2 * nv)),
            _const_spec((depth, nk)),
            _const_spec((1, nv)),
        ],
        out_specs=pl.BlockSpec((tm, nv), lambda i: (i, 0)),
        out_shape=jax.ShapeDtypeStruct((m, nv), _MXU_DTYPE),
        scratch_shapes=[
            pltpu.VMEM((tm, 2 * nk + 2 * nv), _F32),
            pltpu.VMEM((HGRN_HEADS, _LANES, HGRN_DK), _F32),
            pltpu.VMEM((HGRN_HEADS, _LANES, HGRN_DK), _F32),
        ],
        compiler_params=_params("arbitrary"),
        name="hgrn_branch",
    )(x, norm_pre.reshape(1, d), w_in_hgrn, lb_logits, nw)


def _merge_kernel(x_ref, npre_ref, wgc_ref, wga_ref, wgh_ref, yc_ref, ya_ref, yh_ref,
                  wbc_ref, wba_ref, wbh_ref, wo_ref, npost_ref, o_ref, h_ref, **steps):
    def partial(h, rows):
        merged = (jax.nn.sigmoid(_dot(h, wgc_ref[...])) * _dot(yc_ref[rows, :], wbc_ref[...])
                  + jax.nn.sigmoid(_dot(h, wga_ref[...])) * _dot(ya_ref[rows, :], wba_ref[...])
                  + jax.nn.sigmoid(_dot(h, wgh_ref[...])) * _dot(yh_ref[rows, :], wbh_ref[...]))
        return _dot(merged.astype(_MXU_DTYPE), wo_ref[...])

    _reduce_steps(x_ref, npre_ref, npost_ref, o_ref, h_ref, partial, scale=1.0, **steps)


def _merge_step(x, norm_pre, w_gates, y_conv, y_attn, y_hgrn, wb_conv, wb_attn, wb_hgrn, w_o, norm_post):
    m, d = x.shape
    tm, tn = _pick(m, _MIX_TM), _pick(d, _MIX_TN)
    nt = d // tn
    nc, na, nh = y_conv.shape[1], y_attn.shape[1], y_hgrn.shape[1]
    return pl.pallas_call(
        functools.partial(_merge_kernel, n_steps=nt, rb=min(tm, _FFN_RB), rb_edge=min(tm, _ROW_BLOCK)),
        grid=(m // tm, nt),
        in_specs=[
            pl.BlockSpec((tm, d), lambda i, j: (i, 0)),
            pl.BlockSpec((1, d), lambda i, j: (0, 0)),
            pl.BlockSpec((d, tn), lambda i, j: (0, j)),
            pl.BlockSpec((d, tn), lambda i, j: (0, j + nt)),
            pl.BlockSpec((d, tn), lambda i, j: (0, j + 2 * nt)),
            pl.BlockSpec((tm, nc), lambda i, j: (i, 0)),
            pl.BlockSpec((tm, na), lambda i, j: (i, 0)),
            pl.BlockSpec((tm, nh), lambda i, j: (i, 0)),
            pl.BlockSpec((nc, tn), lambda i, j: (0, j)),
            pl.BlockSpec((na, tn), lambda i, j: (0, j)),
            pl.BlockSpec((nh, tn), lambda i, j: (0, j)),
            pl.BlockSpec((tn, d), lambda i, j: (j, 0)),
            pl.BlockSpec((1, d), lambda i, j: (0, 0)),
        ],
        out_specs=pl.BlockSpec((tm, d), lambda i, j: (i, 0)),
        out_shape=jax.ShapeDtypeStruct((m, d), _F32),
        scratch_shapes=[pltpu.VMEM((tm, d), _MXU_DTYPE)],
        compiler_params=_params("parallel", "arbitrary"),
        name="merge_step",
    )(x, norm_pre.reshape(1, d), w_gates, w_gates, w_gates, y_conv, y_attn, y_hgrn,
      wb_conv, wb_attn, wb_hgrn, w_o, norm_post.reshape(1, d))


def kernel(x, p, ffn1_norm_pre, ffn1_w_gu, ffn1_w_down, ffn1_norm_post, mix_norm_pre, w_in, conv_w, attn_sinks, hgrn_lb_logits, hgrn_norm, w_branch_conv, w_branch_attn, w_branch_hgrn, w_o, mix_norm_post, ffn2_norm_pre, ffn2_w_gu, ffn2_w_down, ffn2_norm_post, ple_norm_pre, w_ple_gate, w_ple_proj, ple_norm_post):
    batch, seq, d = x.shape
    depth = p.shape[0]
    m = batch * seq
    n_conv = 3 * CONV_WIDTH
    n_attn = (ATTN_HEADS + 2 * ATTN_KV_HEADS) * HEAD_DIM
    n_hgrn = 2 * HGRN_HEADS * (HGRN_DK + HGRN_DV)

    xf = x.reshape(m, d)
    for l in range(depth):
        xf = _ffn_half_step(xf, ffn1_norm_pre[l], _to_mxu(ffn1_w_gu, l), _to_mxu(ffn1_w_down, l), ffn1_norm_post[l])
        w_conv, w_attn, w_hgrn, w_gates = _split_to_mxu(w_in, l, (n_conv, n_attn, n_hgrn, 3 * d))
        y_conv = _conv_branch(xf, mix_norm_pre[l], w_conv, conv_w[l], seq)
        y_attn = _attn_branch(xf, mix_norm_pre[l], w_attn, attn_sinks[l], seq)
        y_hgrn = _hgrn_branch(xf, mix_norm_pre[l], w_hgrn, hgrn_lb_logits, hgrn_norm[l], seq, l)
        xf = _merge_step(xf, mix_norm_pre[l], w_gates, y_conv, y_attn, y_hgrn,
                         _to_mxu(w_branch_conv, l), _to_mxu(w_branch_attn, l), _to_mxu(w_branch_hgrn, l),
                         _to_mxu(w_o, l), mix_norm_post[l])
        xf = _ffn_half_step(xf, ffn2_norm_pre[l], _to_mxu(ffn2_w_gu, l), _to_mxu(ffn2_w_down, l), ffn2_norm_post[l])
        xf = _ple_step(xf, p.reshape(depth, m, -1), l, ple_norm_pre[l], _to_mxu(w_ple_gate, l),
                       _to_mxu(w_ple_proj, l), ple_norm_post[l])
    return xf.reshape(batch, seq, d)
```

```python
import functools
import math

import jax
import jax.numpy as jnp
from jax import lax
from jax.experimental import pallas as pl
from jax.experimental.pallas import tpu as pltpu

_MXU_DTYPE = jnp.bfloat16
_F32 = jnp.float32

NORM_EPS = 1e-6
NEG_BIG = -1e30
LB_FLOOR = 1e-20

CONV_WIDTH = 512
CONV_K = 3
ATTN_HEADS = 16
ATTN_KV_HEADS = 2
HEAD_DIM = 64
ATTN_BLOCK = 128
HGRN_HEADS = 8
HGRN_DK = 128
HGRN_DV = 64

_LANES = 128
_HALF = 64
_LOG2E = math.log2(math.e)

_HGRN_CHUNK = 64
_HGRN_SAFE_LOG_DECAY = 80.0

_FFN_TM = 1024
_FFN_RB = 512
_FFN_TF = 512
_MIX_TM = 512
_MIX_TN = 512
_PLE_TM = 1024
_CONV_TM = 512
_ROW_BLOCK = 256
_CAST_ROWS = 512
_CAST_COLS = 2816
_SPLIT_ROWS = 128
_ATTN_TM = 512
_HGRN_TM = 512

_VMEM_LIMIT = 60 * 1024 * 1024


def _pick(n, pref):
    t = min(n, pref)
    assert n % t == 0, (n, pref)
    return t


def _rmsnorm(xf, w):
    return xf * lax.rsqrt(jnp.mean(xf * xf, axis=-1, keepdims=True) + NORM_EPS) * w


def _reduce_steps(x_ref, npre_ref, npost_ref, o_ref, h_ref, partial, *, n_steps, rb, rb_edge, scale):
    j = pl.program_id(1)
    tm = x_ref.shape[0]

    def run(first, last, rows_per_block):
        w_post = npost_ref[...] * scale
        for r in range(tm // rows_per_block):
            rows = slice(r * rows_per_block, (r + 1) * rows_per_block)
            if first:
                h = _rmsnorm(x_ref[rows, :], npre_ref[...]).astype(h_ref.dtype)
                h_ref[rows, :] = h
            else:
                h = h_ref[rows, :]
            acc = partial(h, rows)
            if not first:
                acc = o_ref[rows, :] + acc
            o_ref[rows, :] = x_ref[rows, :] + _rmsnorm(acc, w_post) if last else acc

    if n_steps == 1:
        run(True, True, rb_edge)
        return
    pl.when(j == 0)(functools.partial(run, True, False, rb_edge))
    if n_steps > 2:
        pl.when(jnp.logical_and(j > 0, j < n_steps - 1))(functools.partial(run, False, False, rb))
    pl.when(j == n_steps - 1)(functools.partial(run, False, True, rb_edge))


def _dot(a, b):
    return jnp.dot(a, b, preferred_element_type=_F32)


def _dot_nt(a, b):
    return lax.dot_general(a, b, (((1,), (1,)), ((), ())), preferred_element_type=_F32)


def _dot_tn(a, b):
    return lax.dot_general(a, b, (((0,), (0,)), ((), ())), preferred_element_type=_F32)


def _params(*sem):
    return pltpu.CompilerParams(dimension_semantics=sem, vmem_limit_bytes=_VMEM_LIMIT)


def _const_spec(shape):
    nd = len(shape)
    return pl.BlockSpec(shape, lambda *_: (0,) * nd, pipeline_mode=pl.Buffered(1))


def _cast_kernel(w_ref, o_ref):
    o_ref[...] = w_ref[...].astype(o_ref.dtype)


def _to_mxu(w, layer):
    _, r, c = w.shape
    tc = max(t for t in range(_LANES, min(c, _CAST_COLS) + 1, _LANES) if c % t == 0)
    tr = _pick(r, _CAST_ROWS)
    return pl.pallas_call(
        _cast_kernel,
        grid=(r // tr, c // tc),
        in_specs=[pl.BlockSpec((None, tr, tc), lambda i, j: (layer, i, j))],
        out_specs=pl.BlockSpec((tr, tc), lambda i, j: (i, j)),
        out_shape=jax.ShapeDtypeStruct((r, c), _MXU_DTYPE),
        compiler_params=_params("parallel", "parallel"),
        name="weight_cast",
    )(w)


def _split_cast_kernel(w_ref, *o_refs):
    col = 0
    for o_ref in o_refs:
        o_ref[...] = w_ref[:, col:col + o_ref.shape[1]].astype(o_ref.dtype)
        col += o_ref.shape[1]


def _split_to_mxu(w, layer, widths):
    _, r, c = w.shape
    assert sum(widths) == c and all(n % _LANES == 0 for n in widths)
    tr = _pick(r, _SPLIT_ROWS)
    return pl.pallas_call(
        _split_cast_kernel,
        grid=(r // tr,),
        in_specs=[pl.BlockSpec((None, tr, c), lambda i: (layer, i, 0))],
        out_specs=[pl.BlockSpec((tr, n), lambda i: (i, 0)) for n in widths],
        out_shape=[jax.ShapeDtypeStruct((r, n), _MXU_DTYPE) for n in widths],
        compiler_params=_params("parallel"),
        name="weight_split_cast",
    )(w)


def _ffn_kernel(x_ref, npre_ref, wg_ref, wu_ref, wd_ref, npost_ref, o_ref, h_ref, **steps):
    def partial(h, rows):
        g = _dot(h, wg_ref[...])
        u = _dot(h, wu_ref[...])
        a = (g * jax.nn.sigmoid(g) * u).astype(_MXU_DTYPE)
        return _dot(a, wd_ref[...])

    _reduce_steps(x_ref, npre_ref, npost_ref, o_ref, h_ref, partial, scale=0.5, **steps)


def _ffn_half_step(x, norm_pre, w_gu, w_down, norm_post):
    m, d = x.shape
    f = w_down.shape[0]
    tm, tf = _pick(m, _FFN_TM), _pick(f, _FFN_TF)
    nf = f // tf
    return pl.pallas_call(
        functools.partial(_ffn_kernel, n_steps=nf, rb=min(tm, _FFN_RB), rb_edge=min(tm, _FFN_RB)),
        grid=(m // tm, nf),
        in_specs=[
            pl.BlockSpec((tm, d), lambda i, j: (i, 0)),
            pl.BlockSpec((1, d), lambda i, j: (0, 0)),
            pl.BlockSpec((d, tf), lambda i, j: (0, j)),
            pl.BlockSpec((d, tf), lambda i, j: (0, j + nf)),
            pl.BlockSpec((tf, d), lambda i, j: (j, 0)),
            pl.BlockSpec((1, d), lambda i, j: (0, 0)),
        ],
        out_specs=pl.BlockSpec((tm, d), lambda i, j: (i, 0)),
        out_shape=jax.ShapeDtypeStruct((m, d), _F32),
        scratch_shapes=[pltpu.VMEM((tm, d), _MXU_DTYPE)],
        compiler_params=_params("parallel", "arbitrary"),
        name="ffn_half_step",
    )(x, norm_pre.reshape(1, d), w_gu, w_gu, w_down, norm_post.reshape(1, d))


def _ple_kernel(x_ref, p_ref, npre_ref, wg_ref, wp_ref, npost_ref, o_ref, *, rb):
    for r in range(x_ref.shape[0] // rb):
        rows = slice(r * rb, (r + 1) * rb)
        x = x_ref[rows, :]
        hp = _rmsnorm(x, npre_ref[...]).astype(_MXU_DTYPE)
        gate = jax.nn.sigmoid(_dot(hp, wg_ref[...]))
        proj = _dot(p_ref[rows, :].astype(_MXU_DTYPE), wp_ref[...])
        o_ref[rows, :] = x + _rmsnorm(gate * proj, npost_ref[...])


def _ple_step(x, p, layer, norm_pre, w_gate, w_proj, norm_post):
    m, d = x.shape
    dp = p.shape[2]
    tm = _pick(m, _PLE_TM)
    return pl.pallas_call(
        functools.partial(_ple_kernel, rb=min(tm, _ROW_BLOCK)),
        grid=(m // tm,),
        in_specs=[
            pl.BlockSpec((tm, d), lambda i: (i, 0)),
            pl.BlockSpec((None, tm, dp), lambda i: (layer, i, 0)),
            _const_spec((1, d)),
            _const_spec((d, d)),
            _const_spec((dp, d)),
            _const_spec((1, d)),
        ],
        out_specs=pl.BlockSpec((tm, d), lambda i: (i, 0)),
        out_shape=jax.ShapeDtypeStruct((m, d), _F32),
        compiler_params=_params("parallel"),
        name="ple_step",
    )(x, p, norm_pre.reshape(1, d), w_gate, w_proj, norm_post.reshape(1, d))


def _conv_kernel(x_ref, npre_ref, w_ref, cw_ref, o_ref, carry_ref, *, seq_tiles, rb):
    first = (pl.program_id(0) % seq_tiles) == 0
    tm = x_ref.shape[0]
    cwid = o_ref.shape[1]

    @pl.when(first)
    def _():
        carry_ref[...] = jnp.zeros_like(carry_ref)

    cw = cw_ref[...]
    row = lax.broadcasted_iota(jnp.int32, (rb, cwid), 0)
    tail = carry_ref[...]
    for r in range(tm // rb):
        rows = slice(r * rb, (r + 1) * rb)
        h = _rmsnorm(x_ref[rows, :], npre_ref[...]).astype(_MXU_DTYPE)
        proj = _dot(h, w_ref[...])
        xin, b_gate, c_gate = proj[:, :cwid], proj[:, cwid:2 * cwid], proj[:, 2 * cwid:]
        u = c_gate * xin
        prev1 = tail[7:8, :]
        prev2 = tail[6:7, :]
        u1 = jnp.where(row == 0, prev1, pltpu.roll(u, 1, 0))
        u2 = jnp.where(row == 0, prev2, jnp.where(row == 1, prev1, pltpu.roll(u, 2, 0)))
        y = cw[2:3, :] * u + cw[1:2, :] * u1 + cw[0:1, :] * u2
        o_ref[rows, :] = (b_gate * y).astype(o_ref.dtype)
        tail = u[rb - 8:, :]
    carry_ref[...] = tail


def _conv_branch(x, norm_pre, w_in_conv, conv_w, seq):
    m, d = x.shape
    cwid = conv_w.shape[1]
    tm = _pick(seq, _CONV_TM)
    cw = jnp.zeros((8, cwid), _F32).at[:CONV_K].set(conv_w)
    return pl.pallas_call(
        functools.partial(_conv_kernel, seq_tiles=seq // tm, rb=min(tm, _ROW_BLOCK)),
        grid=(m // tm,),
        in_specs=[
            pl.BlockSpec((tm, d), lambda i: (i, 0)),
            _const_spec((1, d)),
            _const_spec((d, 3 * cwid)),
            _const_spec((8, cwid)),
        ],
        out_specs=pl.BlockSpec((tm, cwid), lambda i: (i, 0)),
        out_shape=jax.ShapeDtypeStruct((m, cwid), _MXU_DTYPE),
        scratch_shapes=[pltpu.VMEM((8, cwid), _F32)],
        compiler_params=_params("arbitrary"),
        name="conv_branch",
    )(x, norm_pre.reshape(1, d), w_in_conv, cw)


def _attn_kernel(sink_ref, x_ref, npre_ref, w_ref, o_ref, q_s, k_s, v_s, bias_s, *, seq_tiles):
    blk = ATTN_BLOCK
    tm = x_ref.shape[0]
    nq = ATTN_HEADS * HEAD_DIM
    nkv = ATTN_KV_HEADS * HEAD_DIM
    first = (pl.program_id(0) % seq_tiles) == 0
    group = ATTN_HEADS // ATTN_KV_HEADS

    @pl.when(pl.program_id(0) == 0)
    def _():
        qi = lax.broadcasted_iota(jnp.int32, (blk, 2 * blk), 0)
        ki = lax.broadcasted_iota(jnp.int32, (blk, 2 * blk), 1)
        dist = qi + blk - ki
        band = (dist >= 0) & (dist < blk)
        distf = dist.astype(_F32)
        for head in range(ATTN_HEADS):
            slope = _LOG2E * 2.0 ** (-8.0 * (head + 1) / ATTN_HEADS)
            bias_s[0, head] = jnp.where(band, -slope * distf, NEG_BIG)
            bias_s[1, head] = jnp.where(band & (ki >= blk), -slope * distf, NEG_BIG)

    @pl.when(first)
    def _():
        k_s[0:blk, :] = jnp.zeros((blk, nkv), _F32)
        v_s[0:blk, :] = jnp.zeros((blk, nkv), _F32)

    def block_norm(b):
        return _rmsnorm(x_ref[b * blk:(b + 1) * blk, :], npre_ref[...]).astype(_MXU_DTYPE)

    def project(hb, b, part):
        rows = slice(b * blk, (b + 1) * blk)
        if part < 2:
            cols = slice(part * (nq // 2), (part + 1) * (nq // 2))
            q_s[rows, cols] = (_dot(hb, w_ref[:, cols]) * (_LOG2E * HEAD_DIM ** -0.5)).astype(q_s.dtype)
        else:
            kv = _dot(hb, w_ref[:, nq:nq + 2 * nkv])
            k_s[blk + b * blk:blk + (b + 1) * blk, :] = kv[:, :nkv]
            v_s[blk + b * blk:blk + (b + 1) * blk, :] = kv[:, nkv:]

    lo = lax.broadcasted_iota(jnp.int32, (1, _LANES), 1) < _HALF

    def block(b, fills):
        r0 = b * blk
        kc = k_s[r0:r0 + 2 * blk, :]
        vc = v_s[r0:r0 + 2 * blk, :]
        kr = pltpu.roll(kc, _HALF, 1)
        vr = pltpu.roll(vc, _HALF, 1)
        k_dup = [jnp.where(lo, kc, kr).astype(_MXU_DTYPE), jnp.where(lo, kr, kc).astype(_MXU_DTYPE)]
        v_half = [
            [jnp.where(lo, vc, 0.0).astype(_MXU_DTYPE), jnp.where(lo, 0.0, vr).astype(_MXU_DTYPE)],
            [jnp.where(lo, vr, 0.0).astype(_MXU_DTYPE), jnp.where(lo, 0.0, vc).astype(_MXU_DTYPE)],
        ]
        no_prev = first.astype(jnp.int32) if b == 0 else 0
        q_rows = q_s[r0:r0 + blk, :]

        def group_scores(kv):
            qm = []
            for head in range(kv * group, (kv + 1) * group):
                qp = q_rows[:, (head // 2) * _LANES:(head // 2 + 1) * _LANES]
                zero = jnp.zeros_like(qp)
                qm.append(jnp.where(lo, qp, zero) if head % 2 == 0 else jnp.where(lo, zero, qp))
            return _dot_nt(jnp.concatenate(qm, axis=0), k_dup[kv])

        def group_softmax(kv, s_all):
            probs, denoms = [], []
            for i, head in enumerate(range(kv * group, (kv + 1) * group)):
                s = s_all[i * blk:(i + 1) * blk, :] + bias_s[no_prev, head]
                sink = sink_ref[head] * _LOG2E
                mx = jnp.maximum(jnp.max(s, axis=-1, keepdims=True), sink)
                pr = jnp.exp2(s - mx)
                denoms.append(jnp.sum(pr, axis=-1, keepdims=True) + jnp.exp2(sink - mx))
                probs.append(pr.astype(_MXU_DTYPE))
            return probs, denoms

        def group_values(kv, probs, denoms):
            halves = [_dot(jnp.concatenate(probs[half::2], axis=0), v_half[kv][half]) for half in range(2)]
            for j in range(group // 2):
                rows = slice(j * blk, (j + 1) * blk)
                pair = kv * (group // 2) + j
                o = halves[0][rows, :] / denoms[2 * j] + halves[1][rows, :] / denoms[2 * j + 1]
                o_ref[r0:r0 + blk, pair * _LANES:(pair + 1) * _LANES] = o.astype(o_ref.dtype)

        fills[0]()
        s0 = group_scores(0)
        s1 = group_scores(1)
        fills[1]()
        sm0 = group_softmax(0, s0)
        group_values(0, *sm0)
        fills[2]()
        sm1 = group_softmax(1, s1)
        group_values(1, *sm1)

    n_blk = tm // blk
    hb = block_norm(0)
    for part in range(3):
        project(hb, 0, part)
    for b in range(n_blk):
        fills = [lambda: None] * 3
        if b + 1 < n_blk:
            hb = block_norm(b + 1)
            fills = [functools.partial(project, hb, b + 1, part) for part in range(3)]
        block(b, fills)
    k_s[0:blk, :] = k_s[tm:tm + blk, :]
    v_s[0:blk, :] = v_s[tm:tm + blk, :]


def _attn_branch(x, norm_pre, w_in_attn, sinks, seq):
    m, d = x.shape
    nq = ATTN_HEADS * HEAD_DIM
    nkv = ATTN_KV_HEADS * HEAD_DIM
    tm = _pick(seq, _ATTN_TM)
    assert tm % ATTN_BLOCK == 0 and nkv == _LANES
    return pl.pallas_call(
        functools.partial(_attn_kernel, seq_tiles=seq // tm),
        grid=(m // tm,),
        in_specs=[
            pl.BlockSpec(memory_space=pltpu.SMEM),
            pl.BlockSpec((tm, d), lambda i: (i, 0)),
            _const_spec((1, d)),
            _const_spec((d, nq + 2 * nkv)),
        ],
        out_specs=pl.BlockSpec((tm, nq), lambda i: (i, 0)),
        out_shape=jax.ShapeDtypeStruct((m, nq), _MXU_DTYPE),
        scratch_shapes=[
            pltpu.VMEM((tm, nq), _MXU_DTYPE),
            pltpu.VMEM((tm + ATTN_BLOCK, nkv), _F32),
            pltpu.VMEM((tm + ATTN_BLOCK, nkv), _F32),
            pltpu.VMEM((2, ATTN_HEADS, ATTN_BLOCK, 2 * ATTN_BLOCK), _F32),
        ],
        compiler_params=_params("arbitrary"),
        name="attn_branch",
    )(sinks, x, norm_pre.reshape(1, d), w_in_attn)


def _hgrn_kernel(x_ref, npre_ref, w_ref, lbl_ref, nw_ref, o_ref, pr_s, st_s, bak_s, *, seq_tiles, layer):
    ch = _HGRN_CHUNK
    tm = x_ref.shape[0]
    nk = HGRN_HEADS * HGRN_DK
    nv = HGRN_HEADS * HGRN_DV
    first = (pl.program_id(0) % seq_tiles) == 0

    @pl.when(first)
    def _():
        st_s[...] = jnp.zeros_like(st_s)

    lg = lbl_ref[...]
    e = jnp.exp(lg - jnp.max(lg, axis=0, keepdims=True))
    sm = e / jnp.sum(e, axis=0, keepdims=True)
    lb = jnp.zeros((1, nk), _F32)
    for prev in range(layer):
        lb = lb + sm[prev:prev + 1, :]
    lb_floor = jnp.maximum(lb, LB_FLOOR)
    one_minus_lb = 1.0 - lb

    lo = lax.broadcasted_iota(jnp.int32, (1, _LANES), 1) < _HALF
    rid = lax.broadcasted_iota(jnp.int32, (ch, nk), 0)
    tril = lax.broadcasted_iota(jnp.int32, (ch, ch), 0) >= lax.broadcasted_iota(jnp.int32, (ch, ch), 1)
    own = ((lax.broadcasted_iota(jnp.int32, (_LANES, 2 * HGRN_DK), 0) < _HALF)
           == (lax.broadcasted_iota(jnp.int32, (_LANES, 2 * HGRN_DK), 1) < HGRN_DK))
    nw = nw_ref[...]

    def chunk(c, a_min, *, exact, fill=(None, None)):
        rows = slice(c * ch, (c + 1) * ch) if isinstance(c, int) else pl.ds(pl.multiple_of(c * ch, ch), ch)
        if fill[0] is not None:
            fill[0]()
        q = pr_s[rows, 0:nk]
        z = pr_s[rows, nk:2 * nk]
        qs = q * jax.nn.sigmoid(q)
        sig_pos = jax.nn.sigmoid(z)
        sig_neg = 1.0 - sig_pos
        log_f = jnp.log(sig_pos + lb_floor * sig_neg)
        kk = one_minus_lb * sig_neg
        a = log_f
        step = 1
        while step < ch:
            a = a + jnp.where(rid >= step, pltpu.roll(a, step, 0), 0.0)
            step *= 2
        a_last = a[ch - 1:ch, :]
        q_dec = (qs * jnp.exp(a)).astype(_MXU_DTYPE)
        k_end = (kk * jnp.exp(a_last - a)).astype(_MXU_DTYPE)
        d_end = jnp.exp(a_last)
        heads = range(HGRN_HEADS)
        sls = [slice(hd * HGRN_DK, (hd + 1) * HGRN_DK) for hd in heads]
        vps = [pr_s[rows, 2 * nk + pr * _LANES:2 * nk + (pr + 1) * _LANES] for pr in range(HGRN_HEADS // 2)]
        vpm = [v.astype(_MXU_DTYPE) for v in vps]
        if exact:
            sid_v = lax.broadcasted_iota(jnp.int32, (ch, _LANES), 0)

            def query_row(t, intra):
                a_t = jnp.sum(jnp.where(rid == t, a, 0.0), axis=0, keepdims=True)
                q_t = jnp.sum(jnp.where(rid == t, qs, 0.0), axis=0, keepdims=True)
                wts = q_t * jnp.exp(jnp.where(rid <= t, a_t - a, NEG_BIG)) * kk
                out = []
                for hd in heads:
                    col = jnp.sum(wts[:, sls[hd]], axis=-1, keepdims=True)
                    o_t = jnp.sum(col * vps[hd // 2], axis=0, keepdims=True)
                    out.append(jnp.where(sid_v == t, o_t, intra[hd]))
                return tuple(out)

            intra = lax.fori_loop(0, ch, query_row, tuple(jnp.zeros((ch, _LANES), _F32) for _ in heads))
        else:
            k_inv = (kk * jnp.exp(-a)).astype(_MXU_DTYPE)
            att = [_dot_nt(q_dec[:, sls[hd]], k_inv[:, sls[hd]]) for hd in heads]
        pairs = range(HGRN_HEADS // 2)
        psl = [slice(2 * pr * HGRN_DK, 2 * (pr + 1) * HGRN_DK) for pr in pairs]
        upd = [_dot_tn(vpm[pr], k_end[:, psl[pr]]) for pr in pairs]
        states = [st_s[pr] for pr in pairs]
        inter = [_dot_nt(q_dec[:, psl[pr]], states[pr].astype(_MXU_DTYPE)) for pr in pairs]
        if fill[1] is not None:
            fill[1]()
        if not exact:
            att_m = [jnp.where(tril, att[hd], 0.0).astype(_MXU_DTYPE) for hd in heads]
            intra_pair = [_dot(jnp.concatenate(att_m[2 * pr:2 * pr + 2], axis=0), vpm[pr]) for pr in pairs]
            intra = [intra_pair[hd // 2][(hd % 2) * ch:(hd % 2 + 1) * ch, :] for hd in heads]
        for pr in pairs:
            st_s[pr] = states[pr] * d_end[:, psl[pr]] + jnp.where(own, upd[pr], 0.0)
        for pair in pairs:
            gp = pr_s[rows, 2 * nk + nv + pair * _LANES:2 * nk + nv + (pair + 1) * _LANES]
            o = inter[pair] + jnp.where(lo, intra[2 * pair], intra[2 * pair + 1])
            osq = o * o
            ms_lo = jnp.sum(jnp.where(lo, osq, 0.0), axis=-1, keepdims=True) * (1.0 / HGRN_DV)
            ms_hi = jnp.sum(jnp.where(lo, 0.0, osq), axis=-1, keepdims=True) * (1.0 / HGRN_DV)
            rinv = jnp.where(lo, lax.rsqrt(ms_lo + NORM_EPS), lax.rsqrt(ms_hi + NORM_EPS))
            on = o * rinv * nw[:, pair * _LANES:(pair + 1) * _LANES]
            o_ref[rows, pair * _LANES:(pair + 1) * _LANES] = (on * (gp * jax.nn.sigmoid(gp))).astype(o_ref.dtype)
        return jnp.minimum(a_min, a_last)

    bak_s[...] = st_s[...]

    rg = 2 * ch
    n_rg = tm // rg
    n_sl = 4
    scols = (2 * nk + 2 * nv) // n_sl

    def group_norm(g):
        return _rmsnorm(x_ref[g * rg:(g + 1) * rg, :], npre_ref[...]).astype(_MXU_DTYPE)

    def project(hg, g, s):
        cols = slice(s * scols, (s + 1) * scols)
        pr_s[g * rg:(g + 1) * rg, cols] = _dot(hg, w_ref[:, cols])

    hg = group_norm(0)
    for s in range(n_sl):
        project(hg, 0, s)
    a_min = jnp.zeros((1, nk), _F32)
    for c in range(tm // ch):
        g, part = divmod(c, 2)
        fill = (None, None)
        if g + 1 < n_rg:
            if part == 0:
                hg = group_norm(g + 1)
            fill = tuple(functools.partial(project, hg, g + 1, 2 * part + k) for k in range(2))
        a_min = chunk(c, a_min, exact=False, fill=fill)

    @pl.when(jnp.min(a_min) < -_HGRN_SAFE_LOG_DECAY)
    def _():
        st_s[...] = bak_s[...]
        lax.fori_loop(0, tm // ch, functools.partial(chunk, exact=True), jnp.zeros((1, nk), _F32))


def _hgrn_branch(x, norm_pre, w_in_hgrn, lb_logits, norm_w, seq, layer):
    m, d = x.shape
    nk = HGRN_HEADS * HGRN_DK
    nv = HGRN_HEADS * HGRN_DV
    depth = lb_logits.shape[0]
    tm = _pick(seq, _HGRN_TM)
    assert HGRN_DK == _LANES and 2 * HGRN_DV == _LANES and tm % _HGRN_CHUNK == 0
    nw = jnp.tile(norm_w.reshape(1, HGRN_DV), (1, HGRN_HEADS))
    return pl.pallas_call(
        functools.partial(_hgrn_kernel, seq_tiles=seq // tm, layer=layer),
        grid=(m // tm,),
        in_specs=[
            pl.BlockSpec((tm, d), lambda i: (i, 0)),
            _const_spec((1, d)),
            _const_spec((d, 2 * nk + 2 * nv)),
            _const_spec((depth, nk)),
            _const_spec((1, nv)),
        ],
        out_specs=pl.BlockSpec((tm, nv), lambda i: (i, 0)),
        out_shape=jax.ShapeDtypeStruct((m, nv), _MXU_DTYPE),
        scratch_shapes=[
            pltpu.VMEM((tm, 2 * nk + 2 * nv), _F32),
            pltpu.VMEM((HGRN_HEADS // 2, _LANES, 2 * HGRN_DK), _F32),
            pltpu.VMEM((HGRN_HEADS // 2, _LANES, 2 * HGRN_DK), _F32),
        ],
        compiler_params=_params("arbitrary"),
        name="hgrn_branch",
    )(x, norm_pre.reshape(1, d), w_in_hgrn, lb_logits, nw)


def _merge_kernel(x_ref, npre_ref, wgc_ref, wga_ref, wgh_ref, yc_ref, ya_ref, yh_ref,
                  wbc_ref, wba_ref, wbh_ref, wo_ref, npost_ref, o_ref, h_ref, **steps):
    def partial(h, rows):
        merged = (jax.nn.sigmoid(_dot(h, wgc_ref[...])) * _dot(yc_ref[rows, :], wbc_ref[...])
                  + jax.nn.sigmoid(_dot(h, wga_ref[...])) * _dot(ya_ref[rows, :], wba_ref[...])
                  + jax.nn.sigmoid(_dot(h, wgh_ref[...])) * _dot(yh_ref[rows, :], wbh_ref[...]))
        return _dot(merged.astype(_MXU_DTYPE), wo_ref[...])

    _reduce_steps(x_ref, npre_ref, npost_ref, o_ref, h_ref, partial, scale=1.0, **steps)


def _merge_step(x, norm_pre, w_gates, y_conv, y_attn, y_hgrn, wb_conv, wb_attn, wb_hgrn, w_o, norm_post):
    m, d = x.shape
    tm, tn = _pick(m, _MIX_TM), _pick(d, _MIX_TN)
    nt = d // tn
    nc, na, nh = y_conv.shape[1], y_attn.shape[1], y_hgrn.shape[1]
    return pl.pallas_call(
        functools.partial(_merge_kernel, n_steps=nt, rb=min(tm, _FFN_RB), rb_edge=min(tm, _ROW_BLOCK)),
        grid=(m // tm, nt),
        in_specs=[
            pl.BlockSpec((tm, d), lambda i, j: (i, 0)),
            pl.BlockSpec((1, d), lambda i, j: (0, 0)),
            pl.BlockSpec((d, tn), lambda i, j: (0, j)),
            pl.BlockSpec((d, tn), lambda i, j: (0, j + nt)),
            pl.BlockSpec((d, tn), lambda i, j: (0, j + 2 * nt)),
            pl.BlockSpec((tm, nc), lambda i, j: (i, 0)),
            pl.BlockSpec((tm, na), lambda i, j: (i, 0)),
            pl.BlockSpec((tm, nh), lambda i, j: (i, 0)),
            pl.BlockSpec((nc, tn), lambda i, j: (0, j)),
            pl.BlockSpec((na, tn), lambda i, j: (0, j)),
            pl.BlockSpec((nh, tn), lambda i, j: (0, j)),
            pl.BlockSpec((tn, d), lambda i, j: (j, 0)),
            pl.BlockSpec((1, d), lambda i, j: (0, 0)),
        ],
        out_specs=pl.BlockSpec((tm, d), lambda i, j: (i, 0)),
        out_shape=jax.ShapeDtypeStruct((m, d), _F32),
        scratch_shapes=[pltpu.VMEM((tm, d), _MXU_DTYPE)],
        compiler_params=_params("parallel", "arbitrary"),
        name="merge_step",
    )(x, norm_pre.reshape(1, d), w_gates, w_gates, w_gates, y_conv, y_attn, y_hgrn,
      wb_conv, wb_attn, wb_hgrn, w_o, norm_post.reshape(1, d))


def kernel(x, p, ffn1_norm_pre, ffn1_w_gu, ffn1_w_down, ffn1_norm_post, mix_norm_pre, w_in, conv_w, attn_sinks, hgrn_lb_logits, hgrn_norm, w_branch_conv, w_branch_attn, w_branch_hgrn, w_o, mix_norm_post, ffn2_norm_pre, ffn2_w_gu, ffn2_w_down, ffn2_norm_post, ple_norm_pre, w_ple_gate, w_ple_proj, ple_norm_post):
    batch, seq, d = x.shape
    depth = p.shape[0]
    m = batch * seq
    n_conv = 3 * CONV_WIDTH
    n_attn = (ATTN_HEADS + 2 * ATTN_KV_HEADS) * HEAD_DIM
    n_hgrn = 2 * HGRN_HEADS * (HGRN_DK + HGRN_DV)

    xf = x.reshape(m, d)
    for l in range(depth):
        xf = _ffn_half_step(xf, ffn1_norm_pre[l], _to_mxu(ffn1_w_gu, l), _to_mxu(ffn1_w_down, l), ffn1_norm_post[l])
        w_conv, w_attn, w_hgrn, w_gates = _split_to_mxu(w_in, l, (n_conv, n_attn, n_hgrn, 3 * d))
        y_conv = _conv_branch(xf, mix_norm_pre[l], w_conv, conv_w[l], seq)
        y_attn = _attn_branch(xf, mix_norm_pre[l], w_attn, attn_sinks[l], seq)
        y_hgrn = _hgrn_branch(xf, mix_norm_pre[l], w_hgrn, hgrn_lb_logits, hgrn_norm[l], seq, l)
        xf = _merge_step(xf, mix_norm_pre[l], w_gates, y_conv, y_attn, y_hgrn,
                         _to_mxu(w_branch_conv, l), _to_mxu(w_branch_attn, l), _to_mxu(w_branch_hgrn, l),
                         _to_mxu(w_o, l), mix_norm_post[l])
        xf = _ffn_half_step(xf, ffn2_norm_pre[l], _to_mxu(ffn2_w_gu, l), _to_mxu(ffn2_w_down, l), ffn2_norm_post[l])
        xf = _ple_step(xf, p.reshape(depth, m, -1), l, ple_norm_pre[l], _to_mxu(w_ple_gate, l),
                       _to_mxu(w_ple_proj, l), ple_norm_post[l])
    return xf.reshape(batch, seq, d)
```

```python
import functools
import math

import jax
import jax.numpy as jnp
from jax import lax
from jax.experimental import pallas as pl
from jax.experimental.pallas import tpu as pltpu

_MXU_DTYPE = jnp.bfloat16
_F32 = jnp.float32

NORM_EPS = 1e-6
NEG_BIG = -1e30
LB_FLOOR = 1e-20

CONV_WIDTH = 512
CONV_K = 3
ATTN_HEADS = 16
ATTN_KV_HEADS = 2
HEAD_DIM = 64
ATTN_BLOCK = 128
HGRN_HEADS = 8
HGRN_DK = 128
HGRN_DV = 64

_LANES = 128
_HALF = 64
_LOG2E = math.log2(math.e)

_HGRN_CHUNK = 64
_HGRN_SAFE_LOG_DECAY = 80.0

_FFN_TM = 1024
_FFN_RB = 512
_FFN_TF = 512
_MIX_TM = 512
_MIX_TN = 512
_PLE_TM = 1024
_CONV_TM = 512
_ROW_BLOCK = 256
_PLE_COL_CHUNK = 512
_CAST_ROWS = 512
_CAST_COLS = 2816
_SPLIT_ROWS = 128
_CAST_SUBLANES = 16
_SIDE_CAST_BYTES = 512 * 1024
_ATTN_TM = 512
_HGRN_TM = 512

_VMEM_LIMIT = 60 * 1024 * 1024


def _pick(n, pref):
    t = min(n, pref)
    assert n % t == 0, (n, pref)
    return t


def _rmsnorm(xf, w):
    return xf * lax.rsqrt(jnp.mean(xf * xf, axis=-1, keepdims=True) + NORM_EPS) * w


def _reduce_steps(x_ref, npre_ref, npost_ref, o_ref, h_ref, partial, *, n_steps, rb, rb_edge, scale, extra=None):
    j = pl.program_id(1)
    tm = x_ref.shape[0]

    def run(first, last, rows_per_block):
        if extra is not None:
            extra()
        w_post = npost_ref[...] * scale
        for r in range(tm // rows_per_block):
            rows = slice(r * rows_per_block, (r + 1) * rows_per_block)
            if first:
                h = _rmsnorm(x_ref[rows, :], npre_ref[...]).astype(h_ref.dtype)
                h_ref[rows, :] = h
            else:
                h = h_ref[rows, :]
            acc = partial(h, rows)
            if not first:
                acc = o_ref[rows, :] + acc
            o_ref[rows, :] = x_ref[rows, :] + _rmsnorm(acc, w_post) if last else acc

    if n_steps == 1:
        run(True, True, rb_edge)
        return
    pl.when(j == 0)(functools.partial(run, True, False, rb_edge))
    if n_steps > 2:
        pl.when(jnp.logical_and(j > 0, j < n_steps - 1))(functools.partial(run, False, False, rb))
    pl.when(j == n_steps - 1)(functools.partial(run, False, True, rb_edge))


def _dot(a, b):
    return jnp.dot(a, b, preferred_element_type=_F32)


def _dot_nt(a, b):
    return lax.dot_general(a, b, (((1,), (1,)), ((), ())), preferred_element_type=_F32)


def _dot_tn(a, b):
    return lax.dot_general(a, b, (((0,), (0,)), ((), ())), preferred_element_type=_F32)


def _params(*sem):
    return pltpu.CompilerParams(dimension_semantics=sem, vmem_limit_bytes=_VMEM_LIMIT)


def _const_spec(shape):
    nd = len(shape)
    return pl.BlockSpec(shape, lambda *_: (0,) * nd, pipeline_mode=pl.Buffered(1))


def _cast_kernel(w_ref, o_ref):
    o_ref[...] = w_ref[...].astype(o_ref.dtype)


def _to_mxu(w, layer):
    _, r, c = w.shape
    tc = max(t for t in range(_LANES, min(c, _CAST_COLS) + 1, _LANES) if c % t == 0)
    tr = _pick(r, _CAST_ROWS)
    return pl.pallas_call(
        _cast_kernel,
        grid=(r // tr, c // tc),
        in_specs=[pl.BlockSpec((None, tr, tc), lambda i, j: (layer, i, j))],
        out_specs=pl.BlockSpec((tr, tc), lambda i, j: (i, j)),
        out_shape=jax.ShapeDtypeStruct((r, c), _MXU_DTYPE),
        compiler_params=_params("parallel", "parallel"),
        name="weight_cast",
    )(w)


def _split_cast_kernel(w_ref, *o_refs):
    col = 0
    for o_ref in o_refs:
        o_ref[...] = w_ref[:, col:col + o_ref.shape[1]].astype(o_ref.dtype)
        col += o_ref.shape[1]


def _split_to_mxu(w, layer, widths):
    _, r, c = w.shape
    assert sum(widths) == c and all(n % _LANES == 0 for n in widths)
    tr = _pick(r, _SPLIT_ROWS)
    return pl.pallas_call(
        _split_cast_kernel,
        grid=(r // tr,),
        in_specs=[pl.BlockSpec((None, tr, c), lambda i: (layer, i, 0))],
        out_specs=[pl.BlockSpec((tr, n), lambda i: (i, 0)) for n in widths],
        out_shape=[jax.ShapeDtypeStruct((r, n), _MXU_DTYPE) for n in widths],
        compiler_params=_params("parallel"),
        name="weight_split_cast",
    )(w)


def _ffn_kernel(x_ref, npre_ref, wg_ref, wu_ref, wd_ref, npost_ref, *rest, n_side, **steps):
    side_src, (o_ref, *side_dst, h_ref) = rest[:n_side], rest[n_side:]

    def partial(h, rows):
        g = _dot(h, wg_ref[...])
        u = _dot(h, wu_ref[...])
        a = (g * jax.nn.sigmoid(g) * u).astype(_MXU_DTYPE)
        return _dot(a, wd_ref[...])

    def convert_later_weights():
        for src, dst in zip(side_src, side_dst):
            dst[...] = src[...].astype(dst.dtype)

    _reduce_steps(x_ref, npre_ref, npost_ref, o_ref, h_ref, partial, scale=0.5,
                  extra=convert_later_weights if n_side else None, **steps)


def _side_cast_plan(w, layer, n_tiles, n_steps):
    _, r, c = w.shape
    for (br, bc), index in (((r // n_tiles, c // n_steps), lambda i, j: (i, j)),
                            ((r // (n_tiles * n_steps), c), lambda i, j: (i * n_steps + j, 0))):
        exact = br * bc * n_tiles * n_steps == r * c
        if exact and br % _CAST_SUBLANES == 0 and bc % _LANES == 0 and br * bc * 4 <= _SIDE_CAST_BYTES:
            return (pl.BlockSpec((None, br, bc), lambda i, j, index=index: (layer,) + index(i, j)),
                    pl.BlockSpec((br, bc), index),
                    jax.ShapeDtypeStruct((r, c), _MXU_DTYPE))
    return None


def _ffn_half_step(x, norm_pre, w_gu, w_down, norm_post, later=()):
    m, d = x.shape
    f = w_down.shape[0]
    tm, tf = _pick(m, _FFN_TM), _pick(f, _FFN_TF)
    nf = f // tf
    plans = [_side_cast_plan(w, layer, m // tm, nf) for w, layer in later]
    fused = [(w, plan) for (w, _), plan in zip(later, plans) if plan is not None]
    outs = pl.pallas_call(
        functools.partial(_ffn_kernel, n_side=len(fused), n_steps=nf, rb=min(tm, _FFN_RB), rb_edge=min(tm, _FFN_RB)),
        grid=(m // tm, nf),
        in_specs=[
            pl.BlockSpec((tm, d), lambda i, j: (i, 0)),
            pl.BlockSpec((1, d), lambda i, j: (0, 0)),
            pl.BlockSpec((d, tf), lambda i, j: (0, j)),
            pl.BlockSpec((d, tf), lambda i, j: (0, j + nf)),
            pl.BlockSpec((tf, d), lambda i, j: (j, 0)),
            pl.BlockSpec((1, d), lambda i, j: (0, 0)),
        ] + [plan[0] for _, plan in fused],
        out_specs=[pl.BlockSpec((tm, d), lambda i, j: (i, 0))] + [plan[1] for _, plan in fused],
        out_shape=[jax.ShapeDtypeStruct((m, d), _F32)] + [plan[2] for _, plan in fused],
        scratch_shapes=[pltpu.VMEM((tm, d), _MXU_DTYPE)],
        compiler_params=_params("parallel", "arbitrary"),
        name="ffn_half_step",
    )(x, norm_pre.reshape(1, d), w_gu, w_gu, w_down, norm_post.reshape(1, d), *[w for w, _ in fused])
    fused_out = iter(outs[1:])
    converted = [next(fused_out) if plan is not None else _to_mxu(w, layer) for (w, layer), plan in zip(later, plans)]
    return outs[0], converted


def _ple_kernel(x_ref, p_ref, npre_ref, wg_ref, wp_ref, npost_ref, o_ref, *, rb):
    d = x_ref.shape[1]
    nc = _PLE_COL_CHUNK
    for r in range(x_ref.shape[0] // rb):
        rows = slice(r * rb, (r + 1) * rb)
        hp = _rmsnorm(x_ref[rows, :], npre_ref[...]).astype(_MXU_DTYPE)
        pp = p_ref[rows, :].astype(_MXU_DTYPE)
        ssq = jnp.zeros((rb, 1), _F32)
        for c in range(d // nc):
            cols = slice(c * nc, (c + 1) * nc)
            v = jax.nn.sigmoid(_dot(hp, wg_ref[:, cols])) * _dot(pp, wp_ref[:, cols])
            ssq = ssq + jnp.sum(v * v, axis=-1, keepdims=True)
            o_ref[rows, cols] = v
        rinv = lax.rsqrt(ssq * (1.0 / d) + NORM_EPS)
        o_ref[rows, :] = x_ref[rows, :] + o_ref[rows, :] * rinv * npost_ref[...]


def _ple_step(x, p, layer, norm_pre, w_gate, w_proj, norm_post):
    m, d = x.shape
    dp = p.shape[2]
    tm = _pick(m, _PLE_TM)
    return pl.pallas_call(
        functools.partial(_ple_kernel, rb=min(tm, _ROW_BLOCK)),
        grid=(m // tm,),
        in_specs=[
            pl.BlockSpec((tm, d), lambda i: (i, 0)),
            pl.BlockSpec((None, tm, dp), lambda i: (layer, i, 0)),
            _const_spec((1, d)),
            _const_spec((d, d)),
            _const_spec((dp, d)),
            _const_spec((1, d)),
        ],
        out_specs=pl.BlockSpec((tm, d), lambda i: (i, 0)),
        out_shape=jax.ShapeDtypeStruct((m, d), _F32),
        compiler_params=_params("parallel"),
        name="ple_step",
    )(x, p, norm_pre.reshape(1, d), w_gate, w_proj, norm_post.reshape(1, d))


def _conv_kernel(x_ref, npre_ref, w_ref, cw_ref, o_ref, carry_ref, *, seq_tiles, rb):
    first = (pl.program_id(0) % seq_tiles) == 0
    tm = x_ref.shape[0]
    cwid = o_ref.shape[1]

    @pl.when(first)
    def _():
        carry_ref[...] = jnp.zeros_like(carry_ref)

    cw = cw_ref[...]
    row = lax.broadcasted_iota(jnp.int32, (rb, cwid), 0)
    tail = carry_ref[...]
    for r in range(tm // rb):
        rows = slice(r * rb, (r + 1) * rb)
        h = _rmsnorm(x_ref[rows, :], npre_ref[...]).astype(_MXU_DTYPE)
        proj = _dot(h, w_ref[...])
        xin, b_gate, c_gate = proj[:, :cwid], proj[:, cwid:2 * cwid], proj[:, 2 * cwid:]
        u = c_gate * xin
        prev1 = tail[7:8, :]
        prev2 = tail[6:7, :]
        u1 = jnp.where(row == 0, prev1, pltpu.roll(u, 1, 0))
        u2 = jnp.where(row == 0, prev2, jnp.where(row == 1, prev1, pltpu.roll(u, 2, 0)))
        y = cw[2:3, :] * u + cw[1:2, :] * u1 + cw[0:1, :] * u2
        o_ref[rows, :] = (b_gate * y).astype(o_ref.dtype)
        tail = u[rb - 8:, :]
    carry_ref[...] = tail


def _conv_branch(x, norm_pre, w_in_conv, conv_w, seq):
    m, d = x.shape
    cwid = conv_w.shape[1]
    tm = _pick(seq, _CONV_TM)
    cw = jnp.zeros((8, cwid), _F32).at[:CONV_K].set(conv_w)
    return pl.pallas_call(
        functools.partial(_conv_kernel, seq_tiles=seq // tm, rb=min(tm, _ROW_BLOCK)),
        grid=(m // tm,),
        in_specs=[
            pl.BlockSpec((tm, d), lambda i: (i, 0)),
            _const_spec((1, d)),
            _const_spec((d, 3 * cwid)),
            _const_spec((8, cwid)),
        ],
        out_specs=pl.BlockSpec((tm, cwid), lambda i: (i, 0)),
        out_shape=jax.ShapeDtypeStruct((m, cwid), _MXU_DTYPE),
        scratch_shapes=[pltpu.VMEM((8, cwid), _F32)],
        compiler_params=_params("arbitrary"),
        name="conv_branch",
    )(x, norm_pre.reshape(1, d), w_in_conv, cw)


def _attn_kernel(sink_ref, x_ref, npre_ref, w_ref, o_ref, q_s, k_s, v_s, bias_s, *, seq_tiles):
    blk = ATTN_BLOCK
    tm = x_ref.shape[0]
    nq = ATTN_HEADS * HEAD_DIM
    nkv = ATTN_KV_HEADS * HEAD_DIM
    first = (pl.program_id(0) % seq_tiles) == 0
    group = ATTN_HEADS // ATTN_KV_HEADS

    @pl.when(pl.program_id(0) == 0)
    def _():
        qi = lax.broadcasted_iota(jnp.int32, (blk, 2 * blk), 0)
        ki = lax.broadcasted_iota(jnp.int32, (blk, 2 * blk), 1)
        dist = qi + blk - ki
        band = (dist >= 0) & (dist < blk)
        distf = dist.astype(_F32)
        for head in range(ATTN_HEADS):
            slope = _LOG2E * 2.0 ** (-8.0 * (head + 1) / ATTN_HEADS)
            bias_s[0, head] = jnp.where(band, -slope * distf, NEG_BIG)
            bias_s[1, head] = jnp.where(band & (ki >= blk), -slope * distf, NEG_BIG)

    @pl.when(first)
    def _():
        k_s[0:blk, :] = jnp.zeros((blk, nkv), _F32)
        v_s[0:blk, :] = jnp.zeros((blk, nkv), _F32)

    def block_norm(b):
        return _rmsnorm(x_ref[b * blk:(b + 1) * blk, :], npre_ref[...]).astype(_MXU_DTYPE)

    def project(hb, b, part):
        rows = slice(b * blk, (b + 1) * blk)
        if part < 2:
            cols = slice(part * (nq // 2), (part + 1) * (nq // 2))
            q_s[rows, cols] = (_dot(hb, w_ref[:, cols]) * (_LOG2E * HEAD_DIM ** -0.5)).astype(q_s.dtype)
        else:
            kv = _dot(hb, w_ref[:, nq:nq + 2 * nkv])
            k_s[blk + b * blk:blk + (b + 1) * blk, :] = kv[:, :nkv]
            v_s[blk + b * blk:blk + (b + 1) * blk, :] = kv[:, nkv:]

    lo = lax.broadcasted_iota(jnp.int32, (1, _LANES), 1) < _HALF

    def block(b, fills):
        r0 = b * blk
        kc = k_s[r0:r0 + 2 * blk, :]
        vc = v_s[r0:r0 + 2 * blk, :]
        kr = pltpu.roll(kc, _HALF, 1)
        vr = pltpu.roll(vc, _HALF, 1)
        k_dup = [jnp.where(lo, kc, kr).astype(_MXU_DTYPE), jnp.where(lo, kr, kc).astype(_MXU_DTYPE)]
        v_half = [
            [jnp.where(lo, vc, 0.0).astype(_MXU_DTYPE), jnp.where(lo, 0.0, vr).astype(_MXU_DTYPE)],
            [jnp.where(lo, vr, 0.0).astype(_MXU_DTYPE), jnp.where(lo, 0.0, vc).astype(_MXU_DTYPE)],
        ]
        no_prev = first.astype(jnp.int32) if b == 0 else 0
        q_rows = q_s[r0:r0 + blk, :]

        def group_scores(kv):
            qm = []
            for head in range(kv * group, (kv + 1) * group):
                qp = q_rows[:, (head // 2) * _LANES:(head // 2 + 1) * _LANES]
                zero = jnp.zeros_like(qp)
                qm.append(jnp.where(lo, qp, zero) if head % 2 == 0 else jnp.where(lo, zero, qp))
            return _dot_nt(jnp.concatenate(qm, axis=0), k_dup[kv])

        def group_softmax(kv, s_all):
            probs, denoms = [], []
            for i, head in enumerate(range(kv * group, (kv + 1) * group)):
                s = s_all[i * blk:(i + 1) * blk, :] + bias_s[no_prev, head]
                sink = sink_ref[head] * _LOG2E
                mx = jnp.maximum(jnp.max(s, axis=-1, keepdims=True), sink)
                pr = jnp.exp2(s - mx)
                denoms.append(jnp.sum(pr, axis=-1, keepdims=True) + jnp.exp2(sink - mx))
                probs.append(pr.astype(_MXU_DTYPE))
            return probs, denoms

        def group_values(kv, probs, denoms):
            halves = [_dot(jnp.concatenate(probs[half::2], axis=0), v_half[kv][half]) for half in range(2)]
            for j in range(group // 2):
                rows = slice(j * blk, (j + 1) * blk)
                pair = kv * (group // 2) + j
                o = halves[0][rows, :] / denoms[2 * j] + halves[1][rows, :] / denoms[2 * j + 1]
                o_ref[r0:r0 + blk, pair * _LANES:(pair + 1) * _LANES] = o.astype(o_ref.dtype)

        fills[0]()
        s0 = group_scores(0)
        s1 = group_scores(1)
        fills[1]()
        sm0 = group_softmax(0, s0)
        group_values(0, *sm0)
        fills[2]()
        sm1 = group_softmax(1, s1)
        group_values(1, *sm1)

    n_blk = tm // blk
    hb = block_norm(0)
    for part in range(3):
        project(hb, 0, part)
    for b in range(n_blk):
        fills = [lambda: None] * 3
        if b + 1 < n_blk:
            hb = block_norm(b + 1)
            fills = [functools.partial(project, hb, b + 1, part) for part in range(3)]
        block(b, fills)
    k_s[0:blk, :] = k_s[tm:tm + blk, :]
    v_s[0:blk, :] = v_s[tm:tm + blk, :]


def _attn_branch(x, norm_pre, w_in_attn, sinks, seq):
    m, d = x.shape
    nq = ATTN_HEADS * HEAD_DIM
    nkv = ATTN_KV_HEADS * HEAD_DIM
    tm = _pick(seq, _ATTN_TM)
    assert tm % ATTN_BLOCK == 0 and nkv == _LANES
    return pl.pallas_call(
        functools.partial(_attn_kernel, seq_tiles=seq // tm),
        grid=(m // tm,),
        in_specs=[
            pl.BlockSpec(memory_space=pltpu.SMEM),
            pl.BlockSpec((tm, d), lambda i: (i, 0)),
            _const_spec((1, d)),
            _const_spec((d, nq + 2 * nkv)),
        ],
        out_specs=pl.BlockSpec((tm, nq), lambda i: (i, 0)),
        out_shape=jax.ShapeDtypeStruct((m, nq), _MXU_DTYPE),
        scratch_shapes=[
            pltpu.VMEM((tm, nq), _MXU_DTYPE),
            pltpu.VMEM((tm + ATTN_BLOCK, nkv), _F32),
            pltpu.VMEM((tm + ATTN_BLOCK, nkv), _F32),
            pltpu.VMEM((2, ATTN_HEADS, ATTN_BLOCK, 2 * ATTN_BLOCK), _F32),
        ],
        compiler_params=_params("arbitrary"),
        name="attn_branch",
    )(sinks, x, norm_pre.reshape(1, d), w_in_attn)


def _hgrn_kernel(x_ref, npre_ref, w_ref, lbl_ref, nw_ref, o_ref, pr_s, st_s, bak_s, *, seq_tiles, layer):
    ch = _HGRN_CHUNK
    tm = x_ref.shape[0]
    nk = HGRN_HEADS * HGRN_DK
    nv = HGRN_HEADS * HGRN_DV
    first = (pl.program_id(0) % seq_tiles) == 0

    @pl.when(first)
    def _():
        st_s[...] = jnp.zeros_like(st_s)

    lg = lbl_ref[...]
    e = jnp.exp(lg - jnp.max(lg, axis=0, keepdims=True))
    sm = e / jnp.sum(e, axis=0, keepdims=True)
    lb = jnp.zeros((1, nk), _F32)
    for prev in range(layer):
        lb = lb + sm[prev:prev + 1, :]
    lb_floor = jnp.maximum(lb, LB_FLOOR)
    one_minus_lb = 1.0 - lb

    lo = lax.broadcasted_iota(jnp.int32, (1, _LANES), 1) < _HALF
    rid = lax.broadcasted_iota(jnp.int32, (ch, nk), 0)
    tril = lax.broadcasted_iota(jnp.int32, (ch, ch), 0) >= lax.broadcasted_iota(jnp.int32, (ch, ch), 1)
    own = ((lax.broadcasted_iota(jnp.int32, (_LANES, 2 * HGRN_DK), 0) < _HALF)
           == (lax.broadcasted_iota(jnp.int32, (_LANES, 2 * HGRN_DK), 1) < HGRN_DK))
    nw = nw_ref[...]

    def chunk(c, a_min, *, exact, fill=(None, None)):
        rows = slice(c * ch, (c + 1) * ch) if isinstance(c, int) else pl.ds(pl.multiple_of(c * ch, ch), ch)
        if fill[0] is not None:
            fill[0]()
        q = pr_s[rows, 0:nk]
        z = pr_s[rows, nk:2 * nk]
        qs = q * jax.nn.sigmoid(q)
        sig_pos = jax.nn.sigmoid(z)
        sig_neg = 1.0 - sig_pos
        log_f = jnp.log(sig_pos + lb_floor * sig_neg)
        kk = one_minus_lb * sig_neg
        a = log_f
        step = 1
        while step < ch:
            a = a + jnp.where(rid >= step, pltpu.roll(a, step, 0), 0.0)
            step *= 2
        a_last = a[ch - 1:ch, :]
        q_dec = (qs * jnp.exp(a)).astype(_MXU_DTYPE)
        k_end = (kk * jnp.exp(a_last - a)).astype(_MXU_DTYPE)
        d_end = jnp.exp(a_last)
        heads = range(HGRN_HEADS)
        sls = [slice(hd * HGRN_DK, (hd + 1) * HGRN_DK) for hd in heads]
        vps = [pr_s[rows, 2 * nk + pr * _LANES:2 * nk + (pr + 1) * _LANES] for pr in range(HGRN_HEADS // 2)]
        vpm = [v.astype(_MXU_DTYPE) for v in vps]
        if exact:
            sid_v = lax.broadcasted_iota(jnp.int32, (ch, _LANES), 0)

            def query_row(t, intra):
                a_t = jnp.sum(jnp.where(rid == t, a, 0.0), axis=0, keepdims=True)
                q_t = jnp.sum(jnp.where(rid == t, qs, 0.0), axis=0, keepdims=True)
                wts = q_t * jnp.exp(jnp.where(rid <= t, a_t - a, NEG_BIG)) * kk
                out = []
                for hd in heads:
                    col = jnp.sum(wts[:, sls[hd]], axis=-1, keepdims=True)
                    o_t = jnp.sum(col * vps[hd // 2], axis=0, keepdims=True)
                    out.append(jnp.where(sid_v == t, o_t, intra[hd]))
                return tuple(out)

            intra = lax.fori_loop(0, ch, query_row, tuple(jnp.zeros((ch, _LANES), _F32) for _ in heads))
        else:
            k_inv = (kk * jnp.exp(-a)).astype(_MXU_DTYPE)
            att = [_dot_nt(q_dec[:, sls[hd]], k_inv[:, sls[hd]]) for hd in heads]
        pairs = range(HGRN_HEADS // 2)
        psl = [slice(2 * pr * HGRN_DK, 2 * (pr + 1) * HGRN_DK) for pr in pairs]
        upd = [_dot_tn(vpm[pr], k_end[:, psl[pr]]) for pr in pairs]
        states = [st_s[pr] for pr in pairs]
        inter = [_dot_nt(q_dec[:, psl[pr]], states[pr].astype(_MXU_DTYPE)) for pr in pairs]
        if fill[1] is not None:
            fill[1]()
        if not exact:
            att_m = [jnp.where(tril, att[hd], 0.0).astype(_MXU_DTYPE) for hd in heads]
            intra_pair = [_dot(jnp.concatenate(att_m[2 * pr:2 * pr + 2], axis=0), vpm[pr]) for pr in pairs]
            intra = [intra_pair[hd // 2][(hd % 2) * ch:(hd % 2 + 1) * ch, :] for hd in heads]
        for pr in pairs:
            st_s[pr] = states[pr] * d_end[:, psl[pr]] + jnp.where(own, upd[pr], 0.0)
        for pair in pairs:
            gp = pr_s[rows, 2 * nk + nv + pair * _LANES:2 * nk + nv + (pair + 1) * _LANES]
            o = inter[pair] + jnp.where(lo, intra[2 * pair], intra[2 * pair + 1])
            osq = o * o
            ms_lo = jnp.sum(jnp.where(lo, osq, 0.0), axis=-1, keepdims=True) * (1.0 / HGRN_DV)
            ms_hi = jnp.sum(jnp.where(lo, 0.0, osq), axis=-1, keepdims=True) * (1.0 / HGRN_DV)
            rinv = jnp.where(lo, lax.rsqrt(ms_lo + NORM_EPS), lax.rsqrt(ms_hi + NORM_EPS))
            on = o * rinv * nw[:, pair * _LANES:(pair + 1) * _LANES]
            o_ref[rows, pair * _LANES:(pair + 1) * _LANES] = (on * (gp * jax.nn.sigmoid(gp))).astype(o_ref.dtype)
        return jnp.minimum(a_min, a_last)

    bak_s[...] = st_s[...]

    rg = 2 * ch
    n_rg = tm // rg
    n_sl = 4
    scols = (2 * nk + 2 * nv) // n_sl

    def group_norm(g):
        return _rmsnorm(x_ref[g * rg:(g + 1) * rg, :], npre_ref[...]).astype(_MXU_DTYPE)

    def project(hg, g, s):
        cols = slice(s * scols, (s + 1) * scols)
        pr_s[g * rg:(g + 1) * rg, cols] = _dot(hg, w_ref[:, cols])

    hg = group_norm(0)
    for s in range(n_sl):
        project(hg, 0, s)
    a_min = jnp.zeros((1, nk), _F32)
    for c in range(tm // ch):
        g, part = divmod(c, 2)
        fill = (None, None)
        if g + 1 < n_rg:
            if part == 0:
                hg = group_norm(g + 1)
            fill = tuple(functools.partial(project, hg, g + 1, 2 * part + k) for k in range(2))
        a_min = chunk(c, a_min, exact=False, fill=fill)

    @pl.when(jnp.min(a_min) < -_HGRN_SAFE_LOG_DECAY)
    def _():
        st_s[...] = bak_s[...]
        lax.fori_loop(0, tm // ch, functools.partial(chunk, exact=True), jnp.zeros((1, nk), _F32))


def _hgrn_branch(x, norm_pre, w_in_hgrn, lb_logits, norm_w, seq, layer):
    m, d = x.shape
    nk = HGRN_HEADS * HGRN_DK
    nv = HGRN_HEADS * HGRN_DV
    depth = lb_logits.shape[0]
    tm = _pick(seq, _HGRN_TM)
    assert HGRN_DK == _LANES and 2 * HGRN_DV == _LANES and tm % _HGRN_CHUNK == 0
    nw = jnp.tile(norm_w.reshape(1, HGRN_DV), (1, HGRN_HEADS))
    return pl.pallas_call(
        functools.partial(_hgrn_kernel, seq_tiles=seq // tm, layer=layer),
        grid=(m // tm,),
        in_specs=[
            pl.BlockSpec((tm, d), lambda i: (i, 0)),
            _const_spec((1, d)),
            _const_spec((d, 2 * nk + 2 * nv)),
            _const_spec((depth, nk)),
            _const_spec((1, nv)),
        ],
        out_specs=pl.BlockSpec((tm, nv), lambda i: (i, 0)),
        out_shape=jax.ShapeDtypeStruct((m, nv), _MXU_DTYPE),
        scratch_shapes=[
            pltpu.VMEM((tm, 2 * nk + 2 * nv), _F32),
            pltpu.VMEM((HGRN_HEADS // 2, _LANES, 2 * HGRN_DK), _F32),
            pltpu.VMEM((HGRN_HEADS // 2, _LANES, 2 * HGRN_DK), _F32),
        ],
        compiler_params=_params("arbitrary"),
        name="hgrn_branch",
    )(x, norm_pre.reshape(1, d), w_in_hgrn, lb_logits, nw)


def _merge_kernel(x_ref, npre_ref, wgc_ref, wga_ref, wgh_ref, yc_ref, ya_ref, yh_ref,
                  wbc_ref, wba_ref, wbh_ref, wo_ref, npost_ref, o_ref, h_ref, **steps):
    def partial(h, rows):
        merged = (jax.nn.sigmoid(_dot(h, wgc_ref[...])) * _dot(yc_ref[rows, :], wbc_ref[...])
                  + jax.nn.sigmoid(_dot(h, wga_ref[...])) * _dot(ya_ref[rows, :], wba_ref[...])
                  + jax.nn.sigmoid(_dot(h, wgh_ref[...])) * _dot(yh_ref[rows, :], wbh_ref[...]))
        return _dot(merged.astype(_MXU_DTYPE), wo_ref[...])

    _reduce_steps(x_ref, npre_ref, npost_ref, o_ref, h_ref, partial, scale=1.0, **steps)


def _merge_step(x, norm_pre, w_gates, y_conv, y_attn, y_hgrn, wb_conv, wb_attn, wb_hgrn, w_o, norm_post):
    m, d = x.shape
    tm, tn = _pick(m, _MIX_TM), _pick(d, _MIX_TN)
    nt = d // tn
    nc, na, nh = y_conv.shape[1], y_attn.shape[1], y_hgrn.shape[1]
    return pl.pallas_call(
        functools.partial(_merge_kernel, n_steps=nt, rb=min(tm, _FFN_RB), rb_edge=min(tm, _ROW_BLOCK)),
        grid=(m // tm, nt),
        in_specs=[
            pl.BlockSpec((tm, d), lambda i, j: (i, 0)),
            pl.BlockSpec((1, d), lambda i, j: (0, 0)),
            pl.BlockSpec((d, tn), lambda i, j: (0, j)),
            pl.BlockSpec((d, tn), lambda i, j: (0, j + nt)),
            pl.BlockSpec((d, tn), lambda i, j: (0, j + 2 * nt)),
            pl.BlockSpec((tm, nc), lambda i, j: (i, 0)),
            pl.BlockSpec((tm, na), lambda i, j: (i, 0)),
            pl.BlockSpec((tm, nh), lambda i, j: (i, 0)),
            pl.BlockSpec((nc, tn), lambda i, j: (0, j)),
            pl.BlockSpec((na, tn), lambda i, j: (0, j)),
            pl.BlockSpec((nh, tn), lambda i, j: (0, j)),
            pl.BlockSpec((tn, d), lambda i, j: (j, 0)),
            pl.BlockSpec((1, d), lambda i, j: (0, 0)),
        ],
        out_specs=pl.BlockSpec((tm, d), lambda i, j: (i, 0)),
        out_shape=jax.ShapeDtypeStruct((m, d), _F32),
        scratch_shapes=[pltpu.VMEM((tm, d), _MXU_DTYPE)],
        compiler_params=_params("parallel", "arbitrary"),
        name="merge_step",
    )(x, norm_pre.reshape(1, d), w_gates, w_gates, w_gates, y_conv, y_attn, y_hgrn,
      wb_conv, wb_attn, wb_hgrn, w_o, norm_post.reshape(1, d))


def kernel(x, p, ffn1_norm_pre, ffn1_w_gu, ffn1_w_down, ffn1_norm_post, mix_norm_pre, w_in, conv_w, attn_sinks, hgrn_lb_logits, hgrn_norm, w_branch_conv, w_branch_attn, w_branch_hgrn, w_o, mix_norm_post, ffn2_norm_pre, ffn2_w_gu, ffn2_w_down, ffn2_norm_post, ple_norm_pre, w_ple_gate, w_ple_proj, ple_norm_post):
    batch, seq, d = x.shape
    depth = p.shape[0]
    m = batch * seq
    n_conv = 3 * CONV_WIDTH
    n_attn = (ATTN_HEADS + 2 * ATTN_KV_HEADS) * HEAD_DIM
    n_hgrn = 2 * HGRN_HEADS * (HGRN_DK + HGRN_DV)

    xf = x.reshape(m, d)
    w_gu, w_down = _to_mxu(ffn1_w_gu, 0), _to_mxu(ffn1_w_down, 0)
    for l in range(depth):
        xf, (w_gu, w_down) = _ffn_half_step(xf, ffn1_norm_pre[l], w_gu, w_down, ffn1_norm_post[l],
                                            later=((ffn2_w_gu, l), (ffn2_w_down, l)))
        w_conv, w_attn, w_hgrn, w_gates = _split_to_mxu(w_in, l, (n_conv, n_attn, n_hgrn, 3 * d))
        y_conv = _conv_branch(xf, mix_norm_pre[l], w_conv, conv_w[l], seq)
        y_attn = _attn_branch(xf, mix_norm_pre[l], w_attn, attn_sinks[l], seq)
        y_hgrn = _hgrn_branch(xf, mix_norm_pre[l], w_hgrn, hgrn_lb_logits, hgrn_norm[l], seq, l)
        xf = _merge_step(xf, mix_norm_pre[l], w_gates, y_conv, y_attn, y_hgrn,
                         _to_mxu(w_branch_conv, l), _to_mxu(w_branch_attn, l), _to_mxu(w_branch_hgrn, l),
                         _to_mxu(w_o, l), mix_norm_post[l])
        later = ((ffn1_w_gu, l + 1), (ffn1_w_down, l + 1)) if l + 1 < depth else ()
        xf, later_w = _ffn_half_step(xf, ffn2_norm_pre[l], w_gu, w_down, ffn2_norm_post[l], later=later)
        if later_w:
            w_gu, w_down = later_w
        xf = _ple_step(xf, p.reshape(depth, m, -1), l, ple_norm_pre[l], _to_mxu(w_ple_gate, l),
                       _to_mxu(w_ple_proj, l), ple_norm_post[l])
    return xf.reshape(batch, seq, d)
```
